```python
import math
import jax, jax.numpy as jnp
from jax import lax
import numpy as np

D_MODEL = 1024
BATCH = 2
SEQ = 8192
DEPTH = 2
DEC_BATCH = 8
DEC_SEQ = 32
PAST_LEN = 2048

CHUNK = 64
Q_BLOCK = 128
POOL_WINDOWS = (2, 4, 8, 16)
POOL_GROUPS = 4
POOL_GROUP_DIM = 64
POOL_DIM = POOL_GROUPS * POOL_GROUP_DIM
POOL_STATE = max(POOL_WINDOWS) - 1
SB_HEADS = 4
SB_HEAD_DIM = 64
SB_DIM = SB_HEADS * SB_HEAD_DIM
DIFF_HEADS = 4
DIFF_HALF_DIM = 64
DIFF_HEAD_DIM = 2 * DIFF_HALF_DIM
DIFF_DIM = DIFF_HEADS * DIFF_HEAD_DIM
MIX_DIM = POOL_DIM + SB_DIM + DIFF_DIM
IN_DIM = POOL_DIM + 3 * SB_DIM + 3 * DIFF_DIM
IN_SPLITS = [POOL_DIM,
             POOL_DIM + SB_DIM,
             POOL_DIM + 2 * SB_DIM,
             POOL_DIM + 3 * SB_DIM,
             POOL_DIM + 3 * SB_DIM + DIFF_DIM,
             POOL_DIM + 3 * SB_DIM + 2 * DIFF_DIM]
MEM_LEN = 256
MEM_HEADS = 4
MEM_HEAD_DIM = D_MODEL // MEM_HEADS
D_FF = ((8 * D_MODEL // 3 + 255) // 256) * 256
EPS = 1e-6
NEG_INF = -1e30

kernel_name = "hybrid_pool_stickbreak_diffattn_stream_step"


def _rmsnorm(x, g):
    xf = x.astype(jnp.float32)
    y = xf * lax.rsqrt(jnp.mean(xf * xf, axis=-1, keepdims=True) + EPS)
    return (y * g.astype(jnp.float32)).astype(x.dtype)


def _pool_mix(u, buf, start_pos, pool_w, pool_scale):
    B, T, _ = u.shape
    full = jnp.concatenate([buf, u], axis=1).astype(jnp.float32)
    cs = jnp.concatenate([jnp.zeros_like(full[:, :1]), jnp.cumsum(full, axis=1)], axis=1)
    pos = start_pos + jnp.arange(T)
    means = []
    for g, w in enumerate(POOL_WINDOWS):
        sl = slice(g * POOL_GROUP_DIM, (g + 1) * POOL_GROUP_DIM)
        win = cs[:, POOL_STATE + 1:, sl] - cs[:, POOL_STATE + 1 - w:POOL_STATE + 1 - w + T, sl]
        cnt = jnp.minimum(w, pos + 1).astype(jnp.float32)[None, :, None]
        means.append(win / cnt)
    d = (jnp.concatenate(means, axis=-1) - u.astype(jnp.float32)).astype(u.dtype)
    d = d.reshape(B, T, POOL_GROUPS, POOL_GROUP_DIM)
    y = jnp.einsum('btgc,gcd->btgd', d, pool_w).reshape(B, T, POOL_DIM)
    return y * pool_scale


def _stick_breaking(q, k, v, q_pos, k_pos):
    z = jnp.einsum('bqhd,bkhd->bhqk', q.astype(jnp.float32), k.astype(jnp.float32)) * (SB_HEAD_DIM ** -0.5)
    mask = k_pos[None, :] < q_pos[:, None]
    log_keep = jnp.where(mask, jax.nn.log_sigmoid(-z), 0.0)
    suffix = lax.cumsum(log_keep, axis=3, reverse=True) - log_keep
    a = jnp.where(mask, jnp.exp(jax.nn.log_sigmoid(z) + suffix), 0.0)
    return jnp.einsum('bhqk,bkhd->bqhd', a, v.astype(jnp.float32))


def _diff_attention(q, k, v, q_pos, k_pos, lam):
    B, Tk = k.shape[:2]
    k = k.reshape(B, Tk, DIFF_HEADS, 2, DIFF_HALF_DIM)
    s = jnp.einsum('bqhid,bkhid->bihqk', q.astype(jnp.float32), k.astype(jnp.float32)) * (DIFF_HALF_DIM ** -0.5)
    mask = (k_pos[None, :] // CHUNK) <= (q_pos[:, None] // CHUNK)
    p = jax.nn.softmax(jnp.where(mask, s, NEG_INF), axis=-1)
    a = p[:, 0] - lam * p[:, 1]
    return jnp.einsum('bhqk,bkhe->bqhe', a, v.astype(jnp.float32))


def _sweep(fn, q, q_pos):
    B, S = q.shape[:2]
    nb = S // Q_BLOCK
    qb = q.reshape((B, nb, Q_BLOCK) + q.shape[2:]).swapaxes(0, 1)
    pb = q_pos.reshape(nb, Q_BLOCK)
    out = lax.map(lambda a: fn(a[0], a[1]), (qb, pb))
    return out.swapaxes(0, 1).reshape((B, S) + out.shape[3:])


def _mem_kv(mem, g, wk, wv):
    B, M, _ = mem.shape
    hm = _rmsnorm(mem, g)
    return ((hm @ wk).reshape(B, M, MEM_HEADS, MEM_HEAD_DIM),
            (hm @ wv).reshape(B, M, MEM_HEADS, MEM_HEAD_DIM))


def _mem_attend(h, mk, mv, wq, wo):
    B, T, _ = h.shape
    q = (h @ wq).reshape(B, T, MEM_HEADS, MEM_HEAD_DIM)
    s = jnp.einsum('bqhd,bkhd->bhqk', q.astype(jnp.float32), mk.astype(jnp.float32)) * (MEM_HEAD_DIM ** -0.5)
    p = jax.nn.softmax(s, axis=-1)
    o = jnp.einsum('bhqk,bkhd->bqhd', p, mv.astype(jnp.float32)).astype(h.dtype)
    return o.reshape(B, T, D_MODEL) @ wo


def _layer(x, pool_buf, sb_k_past, sb_v_past, diff_k_past, diff_v_past, mem_k, mem_v, lw, lam_init):
    (g_pre, g_post, w_in, w_out, pool_w, pool_scale, lam_q1, lam_k1, lam_q2, lam_k2,
     diff_g, wq_m, wo_m, w_gate, w_up, w_down) = lw
    B, T, _ = x.shape
    past = 0 if sb_k_past is None else sb_k_past.shape[1]
    q_pos = past + jnp.arange(T)
    k_pos = jnp.arange(past + T)

    h = _rmsnorm(x, g_pre[0])
    u, sq, sk, sv, dq, dk, dv = jnp.split(h @ w_in, IN_SPLITS, axis=-1)
    sq = sq.reshape(B, T, SB_HEADS, SB_HEAD_DIM)
    sk = sk.reshape(B, T, SB_HEADS, SB_HEAD_DIM)
    sv = sv.reshape(B, T, SB_HEADS, SB_HEAD_DIM)
    dq = dq.reshape(B, T, DIFF_HEADS, 2, DIFF_HALF_DIM)
    dk = dk.reshape(B, T, DIFF_HEADS, DIFF_HEAD_DIM)
    dv = dv.reshape(B, T, DIFF_HEADS, DIFF_HEAD_DIM)

    pool_out = _pool_mix(u, pool_buf, past, pool_w, pool_scale)
    new_pool = jnp.concatenate([pool_buf, u], axis=1)[:, -POOL_STATE:]

    if past:
        sk_all = jnp.concatenate([sb_k_past, sk], axis=1)
        sv_all = jnp.concatenate([sb_v_past, sv], axis=1)
        dk_all = jnp.concatenate([diff_k_past, dk], axis=1)
        dv_all = jnp.concatenate([diff_v_past, dv], axis=1)
    else:
        sk_all, sv_all, dk_all, dv_all = sk, sv, dk, dv

    lam = (jnp.exp(jnp.sum(lam_q1.astype(jnp.float32) * lam_k1.astype(jnp.float32)))
           - jnp.exp(jnp.sum(lam_q2.astype(jnp.float32) * lam_k2.astype(jnp.float32))) + lam_init)
    sb_fn = lambda qb, pb: _stick_breaking(qb, sk_all, sv_all, pb, k_pos)
    diff_fn = lambda qb, pb: _diff_attention(qb, dk_all, dv_all, pb, k_pos, lam)
    if T % Q_BLOCK == 0:
        sb_o = _sweep(sb_fn, sq, q_pos)
        diff_o = _sweep(diff_fn, dq, q_pos)
    else:
        sb_o = sb_fn(sq, q_pos)
        diff_o = diff_fn(dq, q_pos)
    diff_o = _rmsnorm(diff_o, diff_g) * (1.0 - lam_init)

    mix = jnp.concatenate([pool_out,
                           sb_o.reshape(B, T, SB_DIM).astype(x.dtype),
                           diff_o.reshape(B, T, DIFF_DIM).astype(x.dtype)], axis=-1)
    x = x + _rmsnorm(mix @ w_out, g_post[0])

    h = _rmsnorm(x, g_pre[1])
    x = x + _rmsnorm(_mem_attend(h, mem_k, mem_v, wq_m, wo_m), g_post[1])

    h = _rmsnorm(x, g_pre[2])
    f = (jax.nn.silu(h @ w_gate) * (h @ w_up)) @ w_down
    x = x + _rmsnorm(f, g_post[2])
    return x, sk, sv, dk, dv, new_pool


def setup_inputs(seed: int = 0) -> dict:
    key = jax.random.key(seed)
    ks = jax.random.split(key, 32)
    nrm = lambda k, shape, s: jax.random.normal(k, shape, jnp.float32) * s
    return {
        "x_prompt": nrm(ks[0], (BATCH, SEQ, D_MODEL), 1.0),
        "x_sample": nrm(ks[1], (DEC_BATCH, DEC_SEQ, D_MODEL), 1.0),
        "cache_sb_k": nrm(ks[2], (DEPTH, DEC_BATCH, PAST_LEN, SB_HEADS, SB_HEAD_DIM), 1.0),
        "cache_sb_v": nrm(ks[3], (DEPTH, DEC_BATCH, PAST_LEN, SB_HEADS, SB_HEAD_DIM), 1.0),
        "cache_diff_k": nrm(ks[4], (DEPTH, DEC_BATCH, PAST_LEN, DIFF_HEADS, DIFF_HEAD_DIM), 1.0),
        "cache_diff_v": nrm(ks[5], (DEPTH, DEC_BATCH, PAST_LEN, DIFF_HEADS, DIFF_HEAD_DIM), 1.0),
        "cache_mem_k": nrm(ks[6], (DEPTH, DEC_BATCH, MEM_LEN, MEM_HEADS, MEM_HEAD_DIM), 1.0),
        "cache_mem_v": nrm(ks[7], (DEPTH, DEC_BATCH, MEM_LEN, MEM_HEADS, MEM_HEAD_DIM), 1.0),
        "state_pool": nrm(ks[8], (DEPTH, DEC_BATCH, POOL_STATE, POOL_DIM), 1.0),
        "mem_prompt": nrm(ks[9], (BATCH, MEM_LEN, D_MODEL), 1.0),
        "g_pre": 1.0 + nrm(ks[10], (DEPTH, 3, D_MODEL), 0.02),
        "g_post": 1.0 + nrm(ks[11], (DEPTH, 3, D_MODEL), 0.02),
        "g_mem": 1.0 + nrm(ks[12], (DEPTH, D_MODEL), 0.02),
        "w_in": nrm(ks[13], (DEPTH, D_MODEL, IN_DIM), D_MODEL ** -0.5),
        "w_out": nrm(ks[14], (DEPTH, MIX_DIM, D_MODEL), MIX_DIM ** -0.5),
        "pool_w": nrm(ks[15], (DEPTH, POOL_GROUPS, POOL_GROUP_DIM, POOL_GROUP_DIM), POOL_GROUP_DIM ** -0.5),
        "pool_scale": 1.0 + nrm(ks[16], (DEPTH, POOL_DIM), 0.1),
        "lam_q1": nrm(ks[17], (DEPTH, DIFF_HALF_DIM), 0.1),
        "lam_k1": nrm(ks[18], (DEPTH, DIFF_HALF_DIM), 0.1),
        "lam_q2": nrm(ks[19], (DEPTH, DIFF_HALF_DIM), 0.1),
        "lam_k2": nrm(ks[20], (DEPTH, DIFF_HALF_DIM), 0.1),
        "diff_g": 1.0 + nrm(ks[21], (DEPTH, DIFF_HEAD_DIM), 0.02),
        "wq_m": nrm(ks[22], (DEPTH, D_MODEL, D_MODEL), D_MODEL ** -0.5),
        "wk_m": nrm(ks[23], (DEPTH, D_MODEL, D_MODEL), D_MODEL ** -0.5),
        "wv_m": nrm(ks[24], (DEPTH, D_MODEL, D_MODEL), D_MODEL ** -0.5),
        "wo_m": nrm(ks[25], (DEPTH, D_MODEL, D_MODEL), D_MODEL ** -0.5),
        "w_gate": nrm(ks[26], (DEPTH, D_MODEL, D_FF), D_MODEL ** -0.5),
        "w_up": nrm(ks[27], (DEPTH, D_MODEL, D_FF), D_MODEL ** -0.5),
        "w_down": nrm(ks[28], (DEPTH, D_FF, D_MODEL), D_FF ** -0.5),
    }


def reference(x_prompt, x_sample, cache_sb_k, cache_sb_v, cache_diff_k, cache_diff_v,
              cache_mem_k, cache_mem_v, state_pool, mem_prompt,
              g_pre, g_post, g_mem, w_in, w_out, pool_w, pool_scale,
              lam_q1, lam_k1, lam_q2, lam_k2, diff_g, wq_m, wk_m, wv_m, wo_m,
              w_gate, w_up, w_down):
    xp, xs = x_prompt, x_sample
    p_sbk, p_sbv, p_dk, p_dv, p_pool, p_mk, p_mv = [], [], [], [], [], [], []
    s_sbk, s_sbv, s_dk, s_dv, s_pool = [], [], [], [], []
    for li in range(DEPTH):
        lam_init = 0.8 - 0.6 * math.exp(-0.3 * li)
        lw = (g_pre[li], g_post[li], w_in[li], w_out[li], pool_w[li], pool_scale[li],
              lam_q1[li], lam_k1[li], lam_q2[li], lam_k2[li], diff_g[li],
              wq_m[li], wo_m[li], w_gate[li], w_up[li], w_down[li])
        mk, mv = _mem_kv(mem_prompt, g_mem[li], wk_m[li], wv_m[li])
        zero_buf = jnp.zeros((xp.shape[0], POOL_STATE, POOL_DIM), xp.dtype)
        xp, sk, sv, dk, dv, npool = _layer(xp, zero_buf, None, None, None, None, mk, mv, lw, lam_init)
        p_sbk.append(sk); p_sbv.append(sv); p_dk.append(dk); p_dv.append(dv)
        p_pool.append(npool); p_mk.append(mk); p_mv.append(mv)
        xs, sk, sv, dk, dv, npool = _layer(xs, state_pool[li], cache_sb_k[li], cache_sb_v[li],
                                           cache_diff_k[li], cache_diff_v[li],
                                           cache_mem_k[li], cache_mem_v[li], lw, lam_init)
        s_sbk.append(sk); s_sbv.append(sv); s_dk.append(dk); s_dv.append(dv); s_pool.append(npool)
    return (xp, xs,
            jnp.stack(p_sbk), jnp.stack(p_sbv), jnp.stack(p_dk), jnp.stack(p_dv),
            jnp.stack(p_pool), jnp.stack(p_mk), jnp.stack(p_mv),
            jnp.stack(s_sbk), jnp.stack(s_sbv), jnp.stack(s_dk), jnp.stack(s_dv),
            jnp.stack(s_pool))
```

```python
import functools
import math

import jax
import jax.numpy as jnp
from jax import lax
from jax.experimental import pallas as pl
from jax.experimental.pallas import tpu as pltpu

F32 = jnp.float32
BF16 = jnp.bfloat16

D_MODEL = 1024
CHUNK = 64
POOL_WINDOWS = (2, 4, 8, 16)
POOL_GROUP_DIM = 64
POOL_DIM = 256
POOL_STATE = 15
SB_HEADS = 4
SB_HEAD_DIM = 64
SB_DIM = 256
DIFF_HEADS = 4
DIFF_HALF_DIM = 64
DIFF_HEAD_DIM = 128
DIFF_DIM = 512
MEM_HEADS = 4
MEM_HEAD_DIM = 256
D_FF = 2816
EPS = 1e-6
NEG_INF = -1e30

LANES = 128
MXU_DIM = 256
VMEM_LIMIT_BYTES = 56 * 1024 * 1024

SB_ZERO_LOG = 110.0


def _cparams(n_grid):
    return pltpu.CompilerParams(
        dimension_semantics=("parallel",) * n_grid,
        vmem_limit_bytes=VMEM_LIMIT_BYTES,
    )


def _rms(x, g):
    return x * lax.rsqrt(jnp.mean(x * x, axis=-1, keepdims=True) + EPS) * g


def _dot(a, b):
    return jnp.dot(a, b, preferred_element_type=F32)


def _dot_nt(a, b):
    return lax.dot_general(a, b, (((1,), (1,)), ((), ())), preferred_element_type=F32)


def _inproj_kernel(x_ref, g_ref, w_ref,
                   u_ref, sk_ref, sv_ref, dk_ref, dv_ref,
                   sqb_ref, skb_ref, svb_ref, dqb_ref, dkb_ref, dvb_ref):
    hb = _rms(x_ref[...], g_ref[...]).astype(BF16)

    def cols(lo, hi):
        return _dot(hb, w_ref[:, lo:hi])

    u_ref[...] = cols(0, 256)
    sqb_ref[...] = (cols(256, 512) * (SB_HEAD_DIM ** -0.5)).astype(BF16)
    sk = cols(512, 768)
    sk_ref[...] = sk
    skb_ref[...] = sk.astype(BF16)
    sv = cols(768, 1024)
    sv_ref[...] = sv
    svb_ref[...] = sv.astype(BF16)
    dqb_ref[...] = (cols(1024, 1536) * (DIFF_HALF_DIM ** -0.5)).astype(BF16)
    dk = cols(1536, 2048)
    dk_ref[...] = dk
    dkb_ref[...] = dk.astype(BF16)
    dv = cols(2048, 2560)
    dv_ref[...] = dv
    dvb_ref[...] = dv.astype(BF16)


def _inproj(x2d, g, w_in_b, tm):
    n = x2d.shape[0]
    assert n % tm == 0
    widths = (256, 256, 256, 512, 512, 256, 256, 256, 512, 512, 512)
    dtypes = (F32,) * 5 + (BF16,) * 6
    out_shape = [jax.ShapeDtypeStruct((n, w), dt) for w, dt in zip(widths, dtypes)]
    out_specs = [pl.BlockSpec((tm, w), lambda i: (i, 0)) for w in widths]
    return pl.pallas_call(
        _inproj_kernel,
        grid=(n // tm,),
        in_specs=[
            pl.BlockSpec((tm, D_MODEL), lambda i: (i, 0)),
            pl.BlockSpec((1, D_MODEL), lambda i: (0, 0)),
            pl.BlockSpec(w_in_b.shape, lambda i: (0, 0)),
        ],
        out_specs=out_specs,
        out_shape=out_shape,
        compiler_params=_cparams(1),
        name="inproj",
    )(x2d, g.reshape(1, D_MODEL), w_in_b)


def _pool_kernel(u_ref, prev_ref, buf_ref, wbd_ref, scale_ref, o_ref, xs_ref, *, tm, start_pos):
    t = pl.program_id(1)
    u = u_ref[0]
    xs_ref[0:16, :] = jnp.where(t == 0, buf_ref[0], prev_ref[0])
    xs_ref[16:, :] = u
    pos = start_pos + t * tm + lax.broadcasted_iota(jnp.int32, (tm, POOL_DIM), 0)
    group = lax.broadcasted_iota(jnp.int32, (tm, POOL_DIM), 1) // POOL_GROUP_DIM
    mean = jnp.zeros((tm, POOL_DIM), F32)
    run = u
    gi = 0
    for back in range(1, max(POOL_WINDOWS)):
        run = run + xs_ref[16 - back:16 - back + tm, :]
        if back + 1 == POOL_WINDOWS[gi]:
            cnt = jnp.minimum(back + 1, pos + 1).astype(F32)
            mean = jnp.where(group == gi, run / cnt, mean)
            gi += 1
    d = (mean - u).astype(BF16)
    o_ref[0] = (_dot(d, wbd_ref[...]) * scale_ref[...]).astype(BF16)


def _pool(u, buf16, wbd_b, pool_scale, tm, start_pos):
    b, t, _ = u.shape
    assert t % tm == 0 and tm % 16 == 0
    r = tm // 16
    return pl.pallas_call(
        functools.partial(_pool_kernel, tm=tm, start_pos=start_pos),
        grid=(b, t // tm),
        in_specs=[
            pl.BlockSpec((1, tm, POOL_DIM), lambda bi, ti: (bi, ti, 0)),
            pl.BlockSpec((1, 16, POOL_DIM), lambda bi, ti: (bi, jnp.maximum(ti * r - 1, 0), 0)),
            pl.BlockSpec((1, 16, POOL_DIM), lambda bi, ti: (bi, 0, 0)),
            pl.BlockSpec((POOL_DIM, POOL_DIM), lambda bi, ti: (0, 0)),
            pl.BlockSpec((1, POOL_DIM), lambda bi, ti: (0, 0)),
        ],
        out_specs=pl.BlockSpec((1, tm, POOL_DIM), lambda bi, ti: (bi, ti, 0)),
        out_shape=jax.ShapeDtypeStruct((b, t, POOL_DIM), BF16),
        scratch_shapes=[pltpu.VMEM((tm + 16, POOL_DIM), F32)],
        compiler_params=_cparams(2),
        name="pool",
    )(u, u, buf16, wbd_b, pool_scale.reshape(1, POOL_DIM))


def _suffix_ones(n):
    r = lax.broadcasted_iota(jnp.int32, (n, n), 0)
    c = lax.broadcasted_iota(jnp.int32, (n, n), 1)
    return jnp.where(r >= c, 1.0, 0.0).astype(BF16)


def _sb_kernel(q_ref, kd_ref, vd_ref, kp_ref, vp_ref, o_ref, acc_ref, ca_ref, cb_ref,
               *, tq, tk, past_blocks):
    qi = pl.program_id(2)
    lane = lax.broadcasted_iota(jnp.int32, (tq, LANES), 1)
    first = lane < SB_HEAD_DIM
    qf = q_ref[0].astype(F32)
    qa = jnp.where(first, qf, 0.0).astype(BF16)
    qb = jnp.where(first, 0.0, qf).astype(BF16)

    def head(qh, k, mask, ones):
        z = _dot_nt(qh, k)
        sp = jnp.maximum(z, 0.0) + jnp.log1p(jnp.exp(-jnp.abs(z)))
        if mask is not None:
            sp = jnp.where(mask, sp, 0.0)
        hi = sp.astype(BF16)
        lo = (sp - hi.astype(F32)).astype(BF16)
        return z, _dot(hi, ones) + _dot(lo, ones)

    def block(k, v, mask, ones):
        ca = ca_ref[...]
        cb = cb_ref[...]
        za, csa = head(qa, k, mask, ones)
        zb, csb = head(qb, k, mask, ones)
        wa = jnp.exp(za - csa - ca)
        wb = jnp.exp(zb - csb - cb)
        if mask is not None:
            wa = jnp.where(mask, wa, 0.0)
            wb = jnp.where(mask, wb, 0.0)
        pva = _dot(wa.astype(BF16), v)
        pvb = _dot(wb.astype(BF16), v)
        acc_ref[...] += jnp.where(first, pva, pvb)
        ca = ca + csa[:, 0:1]
        cb = cb + csb[:, 0:1]
        ca_ref[...] = ca
        cb_ref[...] = cb
        return jnp.min(jnp.minimum(ca, cb))

    acc_ref[...] = jnp.zeros_like(acc_ref)
    ca_ref[...] = jnp.zeros_like(ca_ref)
    cb_ref[...] = jnp.zeros_like(cb_ref)

    row = lax.broadcasted_iota(jnp.int32, (tq, tq), 0)
    col = lax.broadcasted_iota(jnp.int32, (tq, tq), 1)
    cmin = block(kd_ref[0], vd_ref[0], col < row, _suffix_ones(tq))

    ones_p = _suffix_ones(tk)

    def cond(state):
        j, cm = state
        return jnp.logical_and(j >= 0, cm <= SB_ZERO_LOG)

    def body(state):
        j, _ = state
        start = pl.multiple_of(j * tk, tk)
        k = kp_ref[0, pl.ds(start, tk), :].astype(BF16)
        v = vp_ref[0, pl.ds(start, tk), :].astype(BF16)
        return j - 1, block(k, v, None, ones_p)

    lax.while_loop(cond, body, (past_blocks(qi) - 1, cmin))
    o_ref[0] = acc_ref[...].astype(BF16)


def _sb_attention(qb, kb, vb, k_past, v_past, tq, tk, from_self):
    b, t, _ = qb.shape
    p = k_past.shape[1]
    assert t % tq == 0 and p % tk == 0
    if from_self:
        assert tq % tk == 0
        past_blocks = lambda qi: qi * (tq // tk)
    else:
        past_blocks = lambda qi: jnp.int32(p // tk)
    tile = pl.BlockSpec((1, tq, LANES), lambda bi, hi, qi: (bi, qi, hi))
    full = pl.BlockSpec((1, p, LANES), lambda bi, hi, qi: (bi, 0, hi))
    return pl.pallas_call(
        functools.partial(_sb_kernel, tq=tq, tk=tk, past_blocks=past_blocks),
        grid=(b, SB_DIM // LANES, t // tq),
        in_specs=[tile, tile, tile, full, full],
        out_specs=tile,
        out_shape=jax.ShapeDtypeStruct((b, t, SB_DIM), BF16),
        scratch_shapes=[
            pltpu.VMEM((tq, LANES), F32),
            pltpu.VMEM((tq, 1), F32),
            pltpu.VMEM((tq, 1), F32),
        ],
        compiler_params=_cparams(3),
        name="sb_attn",
    )(qb, kb, vb, k_past, v_past)


def _diff_kernel(q_ref, kd_ref, vd_ref, kp_ref, vp_ref,
                 lq1_ref, lk1_ref, lq2_ref, lk2_ref, g_ref, o_ref,
                 m1_ref, l1_ref, a1_ref, m2_ref, l2_ref, a2_ref,
                 *, tq, tk, past_blocks, q_pos0, lam_init):
    qi = pl.program_id(2)
    lane = lax.broadcasted_iota(jnp.int32, (tq, LANES), 1)
    first = lane < DIFF_HALF_DIM
    qf = q_ref[0].astype(F32)
    q1 = jnp.where(first, qf, 0.0).astype(BF16)
    q2 = jnp.where(first, 0.0, qf).astype(BF16)

    qpos = q_pos0(qi) + lax.broadcasted_iota(jnp.int32, (tq, tq), 0)
    kpos = q_pos0(qi) + lax.broadcasted_iota(jnp.int32, (tq, tq), 1)
    mask = (kpos // CHUNK) <= (qpos // CHUNK)
    kd = kd_ref[0]
    vd = vd_ref[0]
    for qh, m_ref, l_ref, a_ref in ((q1, m1_ref, l1_ref, a1_ref), (q2, m2_ref, l2_ref, a2_ref)):
        s = jnp.where(mask, _dot_nt(qh, kd), NEG_INF)
        m = jnp.max(s, axis=-1, keepdims=True)
        p = jnp.exp(s - m)
        m_ref[...] = m
        l_ref[...] = jnp.sum(p, axis=-1, keepdims=True)
        a_ref[...] = _dot(p.astype(BF16), vd)

    def body(j, carry):
        start = pl.multiple_of(j * tk, tk)
        k = kp_ref[0, pl.ds(start, tk), :].astype(BF16)
        v = vp_ref[0, pl.ds(start, tk), :].astype(BF16)
        for qh, m_ref, l_ref, a_ref in ((q1, m1_ref, l1_ref, a1_ref), (q2, m2_ref, l2_ref, a2_ref)):
            s = _dot_nt(qh, k)
            m_old = m_ref[...]
            m_new = jnp.maximum(m_old, jnp.max(s, axis=-1, keepdims=True))
            alpha = jnp.exp(m_old - m_new)
            p = jnp.exp(s - m_new)
            m_ref[...] = m_new
            l_ref[...] = alpha * l_ref[...] + jnp.sum(p, axis=-1, keepdims=True)
            a_ref[...] = alpha * a_ref[...] + _dot(p.astype(BF16), v)
        return carry

    lax.fori_loop(0, past_blocks(qi), body, 0)

    lam = (jnp.exp(jnp.sum(lq1_ref[...] * lk1_ref[...], axis=-1, keepdims=True))
           - jnp.exp(jnp.sum(lq2_ref[...] * lk2_ref[...], axis=-1, keepdims=True)) + lam_init)
    o = a1_ref[...] / l1_ref[...] - lam * (a2_ref[...] / l2_ref[...])
    o_ref[0] = (_rms(o, g_ref[...]) * (1.0 - lam_init)).astype(BF16)


def _diff_attention(qb, kb, vb, k_past, v_past, lam_params, diff_g, tq, tk, from_self, past_len,
                    lam_init):
    b, t, _ = qb.shape
    p = k_past.shape[1]
    assert t % tq == 0 and p % tk == 0
    if from_self:
        assert tq % tk == 0
        past_blocks = lambda qi: qi * (tq // tk)
    else:
        past_blocks = lambda qi: p // tk
    q_pos0 = lambda qi: past_len + qi * tq
    tile = pl.BlockSpec((1, tq, LANES), lambda bi, hi, qi: (bi, qi, hi))
    full = pl.BlockSpec((1, p, LANES), lambda bi, hi, qi: (bi, 0, hi))
    vec64 = pl.BlockSpec((1, DIFF_HALF_DIM), lambda bi, hi, qi: (0, 0))
    vec128 = pl.BlockSpec((1, DIFF_HEAD_DIM), lambda bi, hi, qi: (0, 0))
    return pl.pallas_call(
        functools.partial(_diff_kernel, tq=tq, tk=tk, past_blocks=past_blocks, q_pos0=q_pos0,
                          lam_init=lam_init),
        grid=(b, DIFF_HEADS, t // tq),
        in_specs=[tile, tile, tile, full, full, vec64, vec64, vec64, vec64, vec128],
        out_specs=tile,
        out_shape=jax.ShapeDtypeStruct((b, t, DIFF_DIM), BF16),
        scratch_shapes=[
            pltpu.VMEM((tq, 1), F32), pltpu.VMEM((tq, 1), F32), pltpu.VMEM((tq, LANES), F32),
            pltpu.VMEM((tq, 1), F32), pltpu.VMEM((tq, 1), F32), pltpu.VMEM((tq, LANES), F32),
        ],
        compiler_params=_cparams(3),
        name="diff_attn",
    )(qb, kb, vb, k_past, v_past,
      *[a.reshape(1, DIFF_HALF_DIM) for a in lam_params], diff_g.reshape(1, DIFF_HEAD_DIM))


def _memkv_kernel(m_ref, g_ref, wk_ref, wv_ref, k_ref, v_ref):
    hb = _rms(m_ref[...], g_ref[...]).astype(BF16)
    k_ref[...] = _dot(hb, wk_ref[...])
    v_ref[...] = _dot(hb, wv_ref[...])


def _memkv(mem2d, g, wk_b, wv_b, tm):
    n = mem2d.shape[0]
    assert n % tm == 0
    row = pl.BlockSpec((tm, D_MODEL), lambda i: (i, 0))
    wspec = pl.BlockSpec((D_MODEL, D_MODEL), lambda i: (0, 0))
    return pl.pallas_call(
        _memkv_kernel,
        grid=(n // tm,),
        in_specs=[row, pl.BlockSpec((1, D_MODEL), lambda i: (0, 0)), wspec, wspec],
        out_specs=[row, row],
        out_shape=[jax.ShapeDtypeStruct((n, D_MODEL), F32)] * 2,
        compiler_params=_cparams(1),
        name="memkv",
    )(mem2d, g.reshape(1, D_MODEL), wk_b, wv_b)


def _mix_mem_kernel(x_ref, pool_ref, sb_ref, df_ref, wout_ref, gpost0_ref, gpre1_ref, gpost1_ref,
                    wq_ref, mk_ref, mv_ref, wo_ref, o_ref, att_ref):
    mixed = (_dot(pool_ref[0], wout_ref[0:POOL_DIM, :])
             + _dot(sb_ref[0], wout_ref[POOL_DIM:POOL_DIM + SB_DIM, :])
             + _dot(df_ref[0], wout_ref[POOL_DIM + SB_DIM:, :]))
    x1 = x_ref[0] + _rms(mixed, gpost0_ref[...])

    hb = _rms(x1, gpre1_ref[...]).astype(BF16)
    qm = (_dot(hb, wq_ref[...]) * (MEM_HEAD_DIM ** -0.5)).astype(BF16)
    for h in range(MEM_HEADS):
        sl = slice(h * MEM_HEAD_DIM, (h + 1) * MEM_HEAD_DIM)
        s = _dot_nt(qm[:, sl], mk_ref[0, :, sl].astype(BF16))
        e = jnp.exp(s - jnp.max(s, axis=-1, keepdims=True))
        p = e / jnp.sum(e, axis=-1, keepdims=True)
        att_ref[:, sl] = _dot(p.astype(BF16), mv_ref[0, :, sl].astype(BF16)).astype(BF16)
    y = _dot(att_ref[...], wo_ref[...])
    o_ref[0] = x1 + _rms(y, gpost1_ref[...])


def _mix_mem(x, pool_o, sb_o, df_o, wout_b, g_post0, g_pre1, g_post1, wq_b, mk, mv, wo_b, tm):
    b, t, _ = x.shape
    assert t % tm == 0
    m = mk.shape[1]
    tok = lambda w: pl.BlockSpec((1, tm, w), lambda bi, ti: (bi, ti, 0))
    const = lambda shape: pl.BlockSpec(shape, lambda bi, ti: (0,) * len(shape))
    vec = const((1, D_MODEL))
    memspec = pl.BlockSpec((1, m, D_MODEL), lambda bi, ti: (bi, 0, 0))
    return pl.pallas_call(
        _mix_mem_kernel,
        grid=(b, t // tm),
        in_specs=[tok(D_MODEL), tok(POOL_DIM), tok(SB_DIM), tok(DIFF_DIM),
                  const((D_MODEL, D_MODEL)), vec, vec, vec,
                  const((D_MODEL, D_MODEL)), memspec, memspec, const((D_MODEL, D_MODEL))],
        out_specs=tok(D_MODEL),
        out_shape=jax.ShapeDtypeStruct((b, t, D_MODEL), F32),
        scratch_shapes=[pltpu.VMEM((tm, D_MODEL), BF16)],
        compiler_params=_cparams(2),
        name="mix_mem",
    )(x, pool_o, sb_o, df_o, wout_b, g_post0.reshape(1, -1), g_pre1.reshape(1, -1),
      g_post1.reshape(1, -1), wq_b, mk, mv, wo_b)


def _ffn_kernel(x_ref, gpre_ref, gpost_ref, wg_ref, wu_ref, wd_ref, o_ref, *, fc):
    x = x_ref[...]
    hb = _rms(x, gpre_ref[...]).astype(BF16)
    acc = jnp.zeros(x.shape, F32)
    for c in range(D_FF // fc):
        sl = slice(c * fc, (c + 1) * fc)
        gate = _dot(hb, wg_ref[:, sl])
        up = _dot(hb, wu_ref[:, sl])
        act = (gate / (1.0 + jnp.exp(-gate))) * up
        acc = acc + _dot(act.astype(BF16), wd_ref[sl, :])
    o_ref[...] = x + _rms(acc, gpost_ref[...])


def _ffn(x2d, g_pre2, g_post2, wg_b, wu_b, wd_b, tm, fc=256):
    n = x2d.shape[0]
    assert n % tm == 0 and D_FF % fc == 0
    row = pl.BlockSpec((tm, D_MODEL), lambda i: (i, 0))
    vec = pl.BlockSpec((1, D_MODEL), lambda i: (0, 0))
    return pl.pallas_call(
        functools.partial(_ffn_kernel, fc=fc),
        grid=(n // tm,),
        in_specs=[row, vec, vec,
                  pl.BlockSpec((D_MODEL, D_FF), lambda i: (0, 0)),
                  pl.BlockSpec((D_MODEL, D_FF), lambda i: (0, 0)),
                  pl.BlockSpec((D_FF, D_MODEL), lambda i: (0, 0))],
        out_specs=row,
        out_shape=jax.ShapeDtypeStruct((n, D_MODEL), F32),
        compiler_params=_cparams(1),
        name="ffn",
    )(x2d, g_pre2.reshape(1, -1), g_post2.reshape(1, -1), wg_b, wu_b, wd_b)


def _pick(n, pref):
    return pref if n % pref == 0 else n


def _layer(x, pool_buf, sb_k_past, sb_v_past, diff_k_past, diff_v_past, mem_k, mem_v, lw, lam_init):
    (g_pre, g_post, w_in_b, w_out_b, wbd_b, pool_scale, lam_params, diff_g,
     wq_b, wo_b, wg_b, wu_b, wd_b) = lw
    b, t, _ = x.shape
    n = b * t
    past = 0 if sb_k_past is None else sb_k_past.shape[1]

    (u, sk, sv, dk, dv, sqb, skb, svb, dqb, dkb, dvb) = _inproj(
        x.reshape(n, D_MODEL), g_pre[0], w_in_b, _pick(n, 512))
    r3 = lambda a: a.reshape(b, t, a.shape[-1])
    u, sqb, skb, svb, dqb, dkb, dvb = map(r3, (u, sqb, skb, svb, dqb, dkb, dvb))

    buf16 = jnp.concatenate([jnp.zeros((b, 1, POOL_DIM), F32), pool_buf], axis=1)
    pool_o = _pool(u, buf16, wbd_b, pool_scale, _pick(t, 512), past)
    new_pool = jnp.concatenate([pool_buf, u], axis=1)[:, -POOL_STATE:]

    if past:
        sb_tq, df_tq = t, t
        sb_o = _sb_attention(sqb, skb, svb, sb_k_past.reshape(b, past, SB_DIM),
                             sb_v_past.reshape(b, past, SB_DIM), sb_tq, _pick(past, 256), False)
        df_o = _diff_attention(dqb, dkb, dvb, diff_k_past.reshape(b, past, DIFF_DIM),
                               diff_v_past.reshape(b, past, DIFF_DIM), lam_params, diff_g,
                               df_tq, _pick(past, 512), False, past, lam_init)
    else:
        sb_tq = _pick(t, 256)
        sb_o = _sb_attention(sqb, skb, svb, skb, svb, sb_tq, sb_tq, True)
        df_tq = _pick(t, 512)
        df_o = _diff_attention(dqb, dkb, dvb, dkb, dvb, lam_params, diff_g,
                               df_tq, df_tq, True, 0, lam_init)

    mem_k3 = mem_k.reshape(b, mem_k.shape[1], D_MODEL)
    mem_v3 = mem_v.reshape(b, mem_v.shape[1], D_MODEL)
    x = _mix_mem(x, pool_o, sb_o, df_o, w_out_b, g_post[0], g_pre[1], g_post[1],
                 wq_b, mem_k3, mem_v3, wo_b, _pick(t, 256))
    x = _ffn(x.reshape(n, D_MODEL), g_pre[2], g_post[2], wg_b, wu_b, wd_b,
             _pick(n, 256)).reshape(b, t, D_MODEL)

    sk = sk.reshape(b, t, SB_HEADS, SB_HEAD_DIM)
    sv = sv.reshape(b, t, SB_HEADS, SB_HEAD_DIM)
    dk = dk.reshape(b, t, DIFF_HEADS, DIFF_HEAD_DIM)
    dv = dv.reshape(b, t, DIFF_HEADS, DIFF_HEAD_DIM)
    return x, sk, sv, dk, dv, new_pool


def _block_diag(pool_w):
    g, c, d = pool_w.shape
    out = jnp.zeros((g * c, g * d), pool_w.dtype)
    for i in range(g):
        out = lax.dynamic_update_slice(out, pool_w[i], (i * c, i * d))
    return out


def kernel(x_prompt, x_sample, cache_sb_k, cache_sb_v, cache_diff_k, cache_diff_v, cache_mem_k, cache_mem_v, state_pool, mem_prompt, g_pre, g_post, g_mem, w_in, w_out, pool_w, pool_scale, lam_q1, lam_k1, lam_q2, lam_k2, diff_g, wq_m, wk_m, wv_m, wo_m, w_gate, w_up, w_down):
    depth = w_in.shape[0]
    xp, xs = x_prompt, x_sample
    bp, mem_len = mem_prompt.shape[0], mem_prompt.shape[1]
    p_sbk, p_sbv, p_dk, p_dv, p_pool, p_mk, p_mv = [], [], [], [], [], [], []
    s_sbk, s_sbv, s_dk, s_dv, s_pool = [], [], [], [], []
    for li in range(depth):
        lam_init = 0.8 - 0.6 * math.exp(-0.3 * li)
        cast = lambda w: w[li].astype(BF16)
        lw = (g_pre[li], g_post[li], cast(w_in), cast(w_out), _block_diag(pool_w[li]).astype(BF16),
              pool_scale[li], (lam_q1[li], lam_k1[li], lam_q2[li], lam_k2[li]), diff_g[li],
              cast(wq_m), cast(wo_m), cast(w_gate), cast(w_up), cast(w_down))
        mk2d, mv2d = _memkv(mem_prompt.reshape(bp * mem_len, D_MODEL), g_mem[li],
                            cast(wk_m), cast(wv_m), _pick(bp * mem_len, 256))
        mk = mk2d.reshape(bp, mem_len, MEM_HEADS, MEM_HEAD_DIM)
        mv = mv2d.reshape(bp, mem_len, MEM_HEADS, MEM_HEAD_DIM)
        zero_buf = jnp.zeros((xp.shape[0], POOL_STATE, POOL_DIM), xp.dtype)
        xp, sk, sv, dk, dv, npool = _layer(xp, zero_buf, None, None, None, None, mk, mv, lw, lam_init)
        p_sbk.append(sk); p_sbv.append(sv); p_dk.append(dk); p_dv.append(dv)
        p_pool.append(npool); p_mk.append(mk); p_mv.append(mv)
        xs, sk, sv, dk, dv, npool = _layer(xs, state_pool[li], cache_sb_k[li], cache_sb_v[li],
                                           cache_diff_k[li], cache_diff_v[li],
                                           cache_mem_k[li], cache_mem_v[li], lw, lam_init)
        s_sbk.append(sk); s_sbv.append(sv); s_dk.append(dk); s_dv.append(dv); s_pool.append(npool)
    return (xp, xs,
            jnp.stack(p_sbk), jnp.stack(p_sbv), jnp.stack(p_dk), jnp.stack(p_dv),
            jnp.stack(p_pool), jnp.stack(p_mk), jnp.stack(p_mv),
            jnp.stack(s_sbk), jnp.stack(s_sbv), jnp.stack(s_dk), jnp.stack(s_dv),
            jnp.stack(s_pool))
```

```python
import functools
import math

import jax
import jax.numpy as jnp
from jax import lax
from jax.experimental import pallas as pl
from jax.experimental.pallas import tpu as pltpu

F32 = jnp.float32
BF16 = jnp.bfloat16

D_MODEL = 1024
CHUNK = 64
POOL_WINDOWS = (2, 4, 8, 16)
POOL_GROUP_DIM = 64
POOL_DIM = 256
POOL_STATE = 15
SB_HEADS = 4
SB_HEAD_DIM = 64
SB_DIM = 256
DIFF_HEADS = 4
DIFF_HALF_DIM = 64
DIFF_HEAD_DIM = 128
DIFF_DIM = 512
MEM_HEADS = 4
MEM_HEAD_DIM = 256
D_FF = 2816
EPS = 1e-6
NEG_INF = -1e30
LOG2E = 1.4426950408889634

LANES = 128
MXU_DIM = 256
VMEM_LIMIT_BYTES = 56 * 1024 * 1024

SB_ZERO_LOG = 110.0


def _cparams(n_grid):
    return pltpu.CompilerParams(
        dimension_semantics=("parallel",) * n_grid,
        vmem_limit_bytes=VMEM_LIMIT_BYTES,
    )


def _rms(x, g):
    return x * lax.rsqrt(jnp.mean(x * x, axis=-1, keepdims=True) + EPS) * g


def _dot(a, b):
    return jnp.dot(a, b, preferred_element_type=F32)


def _dot_nt(a, b):
    return lax.dot_general(a, b, (((1,), (1,)), ((), ())), preferred_element_type=F32)


def _inproj_kernel(x_ref, g_ref, w_ref,
                   u_ref, sk_ref, sv_ref, dk_ref, dv_ref,
                   sqb_ref, skb_ref, svb_ref, dqb_ref, dkb_ref, dvb_ref):
    hb = _rms(x_ref[...], g_ref[...]).astype(BF16)

    def cols(lo, hi):
        return _dot(hb, w_ref[:, lo:hi])

    u_ref[...] = cols(0, 256)
    sqb_ref[...] = (cols(256, 512) * (SB_HEAD_DIM ** -0.5)).astype(BF16)
    sk = cols(512, 768)
    sk_ref[...] = sk
    skb_ref[...] = sk.astype(BF16)
    sv = cols(768, 1024)
    sv_ref[...] = sv
    svb_ref[...] = sv.astype(BF16)
    dqb_ref[...] = (cols(1024, 1536) * (LOG2E * DIFF_HALF_DIM ** -0.5)).astype(BF16)
    dk = cols(1536, 2048)
    dk_ref[...] = dk
    dkb_ref[...] = dk.astype(BF16)
    dv = cols(2048, 2560)
    dv_ref[...] = dv
    dvb_ref[...] = dv.astype(BF16)


def _inproj(x2d, g, w_in_b, tm):
    n = x2d.shape[0]
    assert n % tm == 0
    widths = (256, 256, 256, 512, 512, 256, 256, 256, 512, 512, 512)
    dtypes = (F32,) * 5 + (BF16,) * 6
    out_shape = [jax.ShapeDtypeStruct((n, w), dt) for w, dt in zip(widths, dtypes)]
    out_specs = [pl.BlockSpec((tm, w), lambda i: (i, 0)) for w in widths]
    return pl.pallas_call(
        _inproj_kernel,
        grid=(n // tm,),
        in_specs=[
            pl.BlockSpec((tm, D_MODEL), lambda i: (i, 0)),
            pl.BlockSpec((1, D_MODEL), lambda i: (0, 0)),
            pl.BlockSpec(w_in_b.shape, lambda i: (0, 0)),
        ],
        out_specs=out_specs,
        out_shape=out_shape,
        compiler_params=_cparams(1),
        name="inproj",
    )(x2d, g.reshape(1, D_MODEL), w_in_b)


def _pool_kernel(u_ref, prev_ref, buf_ref, wbd_ref, scale_ref, o_ref, xs_ref, *, tm, start_pos):
    t = pl.program_id(1)
    u = u_ref[0]
    xs_ref[0:16, :] = jnp.where(t == 0, buf_ref[0], prev_ref[0])
    xs_ref[16:, :] = u
    pos = start_pos + t * tm + lax.broadcasted_iota(jnp.int32, (tm, POOL_DIM), 0)
    group = lax.broadcasted_iota(jnp.int32, (tm, POOL_DIM), 1) // POOL_GROUP_DIM
    mean = jnp.zeros((tm, POOL_DIM), F32)
    run = u
    gi = 0
    for back in range(1, max(POOL_WINDOWS)):
        run = run + xs_ref[16 - back:16 - back + tm, :]
        if back + 1 == POOL_WINDOWS[gi]:
            cnt = jnp.minimum(back + 1, pos + 1).astype(F32)
            mean = jnp.where(group == gi, run / cnt, mean)
            gi += 1
    d = (mean - u).astype(BF16)
    o_ref[0] = (_dot(d, wbd_ref[...]) * scale_ref[...]).astype(BF16)


def _pool(u, buf16, wbd_b, pool_scale, tm, start_pos):
    b, t, _ = u.shape
    assert t % tm == 0 and tm % 16 == 0
    r = tm // 16
    return pl.pallas_call(
        functools.partial(_pool_kernel, tm=tm, start_pos=start_pos),
        grid=(b, t // tm),
        in_specs=[
            pl.BlockSpec((1, tm, POOL_DIM), lambda bi, ti: (bi, ti, 0)),
            pl.BlockSpec((1, 16, POOL_DIM), lambda bi, ti: (bi, jnp.maximum(ti * r - 1, 0), 0)),
            pl.BlockSpec((1, 16, POOL_DIM), lambda bi, ti: (bi, 0, 0)),
            pl.BlockSpec((POOL_DIM, POOL_DIM), lambda bi, ti: (0, 0)),
            pl.BlockSpec((1, POOL_DIM), lambda bi, ti: (0, 0)),
        ],
        out_specs=pl.BlockSpec((1, tm, POOL_DIM), lambda bi, ti: (bi, ti, 0)),
        out_shape=jax.ShapeDtypeStruct((b, t, POOL_DIM), BF16),
        scratch_shapes=[pltpu.VMEM((tm + 16, POOL_DIM), F32)],
        compiler_params=_cparams(2),
        name="pool",
    )(u, u, buf16, wbd_b, pool_scale.reshape(1, POOL_DIM))


def _suffix_ones(n):
    r = lax.broadcasted_iota(jnp.int32, (n, n), 0)
    c = lax.broadcasted_iota(jnp.int32, (n, n), 1)
    return jnp.where(r >= c, 1.0, 0.0).astype(BF16)


def _sb_kernel(q_ref, kd_ref, vd_ref, kp_ref, vp_ref, o_ref, acc_ref, ca_ref, cb_ref,
               *, tq, tk, past_blocks):
    qi = pl.program_id(2)
    lane = lax.broadcasted_iota(jnp.int32, (tq, LANES), 1)
    first = lane < SB_HEAD_DIM
    qf = q_ref[0].astype(F32)
    qa = jnp.where(first, qf, 0.0).astype(BF16)
    qb = jnp.where(first, 0.0, qf).astype(BF16)

    def head(qh, k, mask, ones):
        z = _dot_nt(qh, k)
        sp = jnp.maximum(z, 0.0) + jnp.log1p(jnp.exp(-jnp.abs(z)))
        if mask is not None:
            sp = jnp.where(mask, sp, 0.0)
        hi = sp.astype(BF16)
        lo = (sp - hi.astype(F32)).astype(BF16)
        return z, _dot(hi, ones) + _dot(lo, ones)

    def block(k, v, mask, ones):
        ca = ca_ref[...]
        cb = cb_ref[...]
        za, csa = head(qa, k, mask, ones)
        zb, csb = head(qb, k, mask, ones)
        wa = jnp.exp(za - csa - ca)
        wb = jnp.exp(zb - csb - cb)
        if mask is not None:
            wa = jnp.where(mask, wa, 0.0)
            wb = jnp.where(mask, wb, 0.0)
        pva = _dot(wa.astype(BF16), v)
        pvb = _dot(wb.astype(BF16), v)
        acc_ref[...] += jnp.where(first, pva, pvb)
        ca = ca + csa[:, 0:1]
        cb = cb + csb[:, 0:1]
        ca_ref[...] = ca
        cb_ref[...] = cb
        return jnp.min(jnp.minimum(ca, cb))

    acc_ref[...] = jnp.zeros_like(acc_ref)
    ca_ref[...] = jnp.zeros_like(ca_ref)
    cb_ref[...] = jnp.zeros_like(cb_ref)

    row = lax.broadcasted_iota(jnp.int32, (tq, tq), 0)
    col = lax.broadcasted_iota(jnp.int32, (tq, tq), 1)
    cmin = block(kd_ref[0], vd_ref[0], col < row, _suffix_ones(tq))

    ones_p = _suffix_ones(tk)

    def cond(state):
        j, cm = state
        return jnp.logical_and(j >= 0, cm <= SB_ZERO_LOG)

    def body(state):
        j, _ = state
        start = pl.multiple_of(j * tk, tk)
        k = kp_ref[0, pl.ds(start, tk), :].astype(BF16)
        v = vp_ref[0, pl.ds(start, tk), :].astype(BF16)
        return j - 1, block(k, v, None, ones_p)

    lax.while_loop(cond, body, (past_blocks(qi) - 1, cmin))
    o_ref[0] = acc_ref[...].astype(BF16)


def _sb_attention(qb, kb, vb, k_past, v_past, tq, tk, from_self):
    b, t, _ = qb.shape
    p = k_past.shape[1]
    assert t % tq == 0 and p % tk == 0
    if from_self:
        assert tq % tk == 0
        past_blocks = lambda qi: qi * (tq // tk)
    else:
        past_blocks = lambda qi: jnp.int32(p // tk)
    tile = pl.BlockSpec((1, tq, LANES), lambda bi, hi, qi: (bi, qi, hi))
    full = pl.BlockSpec((1, p, LANES), lambda bi, hi, qi: (bi, 0, hi))
    return pl.pallas_call(
        functools.partial(_sb_kernel, tq=tq, tk=tk, past_blocks=past_blocks),
        grid=(b, SB_DIM // LANES, t // tq),
        in_specs=[tile, tile, tile, full, full],
        out_specs=tile,
        out_shape=jax.ShapeDtypeStruct((b, t, SB_DIM), BF16),
        scratch_shapes=[
            pltpu.VMEM((tq, LANES), F32),
            pltpu.VMEM((tq, 1), F32),
            pltpu.VMEM((tq, 1), F32),
        ],
        compiler_params=_cparams(3),
        name="sb_attn",
    )(qb, kb, vb, k_past, v_past)


def _exp2_rows(s, m):
    n = s.shape[1] // LANES
    if n == 0:
        return jnp.exp2(s - m[:, :s.shape[1]]).astype(BF16)
    return jnp.concatenate(
        [jnp.exp2(s[:, c * LANES:(c + 1) * LANES] - m).astype(BF16) for c in range(n)], axis=1)


def _diff_kernel(*refs, tq, tk, row_chunk, diag_in_past, past_blocks, q_pos0, lam_init):
    if diag_in_past:
        q_ref, kp_ref, vp_ref = refs[:3]
        kd_ref = vd_ref = None
        rest = refs[3:]
    else:
        q_ref, kd_ref, vd_ref, kp_ref, vp_ref = refs[:5]
        rest = refs[5:]
    (lq1_ref, lk1_ref, lq2_ref, lk2_ref, g_ref, o_ref, m_ref, a_ref, p_ref, alpha_ref) = rest
    qi = pl.program_id(2)
    lane = lax.broadcasted_iota(jnp.int32, (tq, LANES), 1)
    first = lane < DIFF_HALF_DIM
    qf = q_ref[0].astype(F32)
    qs = (jnp.where(first, qf, 0.0).astype(BF16), jnp.where(first, 0.0, qf).astype(BF16))
    n_max = kp_ref.shape[1] // tk
    rows = min(tq, row_chunk)

    def with_ones(v):
        return jnp.concatenate([v, jnp.ones(v.shape, BF16)], axis=1)

    def past(ref, j):
        start = pl.multiple_of(jnp.clip(j, 0, n_max - 1) * tk, tk)
        return ref[0, pl.ds(start, tk), :].astype(BF16)

    def own_mask(r, n_rows, n_keys):
        qpos = q_pos0(qi) + r + lax.broadcasted_iota(jnp.int32, (n_rows, n_keys), 0)
        kpos = q_pos0(qi) + lax.broadcasted_iota(jnp.int32, (n_rows, n_keys), 1)
        return (kpos // CHUNK) <= (qpos // CHUNK)

    def step(k, v_pending, own, pending_own=False):
        v1 = None if v_pending is None else with_ones(v_pending)
        for r in range(0, tq, rows):
            sl = slice(r, r + rows)
            for h in range(2):
                if v1 is not None:
                    nk = r + rows if pending_own else tk
                    alpha = alpha_ref[h, sl]
                    a_ref[h, sl] = (jnp.concatenate([alpha, alpha], axis=1) * a_ref[h, sl]
                                    + _dot(p_ref[h, sl, :nk], v1[:nk]))
                if k is not None:
                    nk = r + rows if own else tk
                    s = _dot_nt(qs[h][sl], k[:nk])
                    if own:
                        s = jnp.where(own_mask(r, rows, nk), s, NEG_INF)
                    m_old = m_ref[h, sl]
                    m_new = jnp.maximum(m_old, jnp.max(s, axis=-1, keepdims=True))
                    alpha_ref[h, sl] = jnp.exp2(m_old - m_new)
                    m_ref[h, sl] = m_new
                    p_ref[h, sl, :nk] = _exp2_rows(s, m_new)

    n_past = past_blocks(qi)
    a_ref[...] = jnp.zeros_like(a_ref)
    if diag_in_past:
        m_ref[...] = jnp.full(m_ref.shape, NEG_INF, F32)
    else:
        for h in range(2):
            s = jnp.where(own_mask(0, tq, tq), _dot_nt(qs[h], kd_ref[0]), NEG_INF)
            m = jnp.broadcast_to(jnp.max(s, axis=-1, keepdims=True), (tq, LANES))
            m_ref[h] = m
            a_ref[h] = _dot(_exp2_rows(s, m), with_ones(vd_ref[0]))
    p_ref[...] = jnp.zeros_like(p_ref)
    alpha_ref[...] = jnp.ones_like(alpha_ref)

    def body(j, carry):
        step(past(kp_ref, j), past(vp_ref, j - 1), False)
        return carry

    lax.fori_loop(0, n_past, body, 0)
    if diag_in_past:
        step(past(kp_ref, n_past), past(vp_ref, n_past - 1), True)
        step(None, past(vp_ref, n_past), False, pending_own=True)
    else:
        step(None, past(vp_ref, n_past - 1), False)

    a1_ref, a2_ref = a_ref.at[0], a_ref.at[1]
    lam = (jnp.exp(jnp.sum(lq1_ref[...] * lk1_ref[...], axis=-1, keepdims=True))
           - jnp.exp(jnp.sum(lq2_ref[...] * lk2_ref[...], axis=-1, keepdims=True)) + lam_init)
    o = (a1_ref[:, :LANES] / a1_ref[:, LANES:]
         - lam * (a2_ref[:, :LANES] / a2_ref[:, LANES:]))
    o_ref[0] = (_rms(o, g_ref[...]) * (1.0 - lam_init)).astype(BF16)


def _diff_attention(qb, kb, vb, k_past, v_past, lam_params, diff_g, tq, tk, from_self, past_len,
                    lam_init):
    b, t, _ = qb.shape
    p = t if from_self else k_past.shape[1]
    assert t % tq == 0 and p % tk == 0
    tile = pl.BlockSpec((1, tq, LANES), lambda bi, hi, qi: (bi, qi, hi))
    full = pl.BlockSpec((1, p, LANES), lambda bi, hi, qi: (bi, 0, hi))
    vec64 = pl.BlockSpec((1, DIFF_HALF_DIM), lambda bi, hi, qi: (0, 0))
    vec128 = pl.BlockSpec((1, DIFF_HEAD_DIM), lambda bi, hi, qi: (0, 0))
    if from_self:
        assert tq == tk
        past_blocks = lambda qi: qi
        kv_specs, kv_args = [full, full], (kb, vb)
    else:
        past_blocks = lambda qi: p // tk
        kv_specs, kv_args = [tile, tile, full, full], (kb, vb, k_past, v_past)
    q_pos0 = lambda qi: past_len + qi * tq
    return pl.pallas_call(
        functools.partial(_diff_kernel, tq=tq, tk=tk, row_chunk=128, diag_in_past=from_self,
                          past_blocks=past_blocks, q_pos0=q_pos0, lam_init=lam_init),
        grid=(b, DIFF_HEADS, t // tq),
        in_specs=[tile] + kv_specs + [vec64, vec64, vec64, vec64, vec128],
        out_specs=tile,
        out_shape=jax.ShapeDtypeStruct((b, t, DIFF_DIM), BF16),
        scratch_shapes=[
            pltpu.VMEM((2, tq, LANES), F32), pltpu.VMEM((2, tq, 2 * LANES), F32),
            pltpu.VMEM((2, tq, tk), BF16), pltpu.VMEM((2, tq, LANES), F32),
        ],
        compiler_params=_cparams(3),
        name="diff_attn",
    )(qb, *kv_args,
      *[a.reshape(1, DIFF_HALF_DIM) for a in lam_params], diff_g.reshape(1, DIFF_HEAD_DIM))


def _memkv_kernel(m_ref, g_ref, wk_ref, wv_ref, k_ref, v_ref):
    hb = _rms(m_ref[...], g_ref[...]).astype(BF16)
    k_ref[...] = _dot(hb, wk_ref[...])
    v_ref[...] = _dot(hb, wv_ref[...])


def _memkv(mem2d, g, wk_b, wv_b, tm):
    n = mem2d.shape[0]
    assert n % tm == 0
    row = pl.BlockSpec((tm, D_MODEL), lambda i: (i, 0))
    wspec = pl.BlockSpec((D_MODEL, D_MODEL), lambda i: (0, 0))
    return pl.pallas_call(
        _memkv_kernel,
        grid=(n // tm,),
        in_specs=[row, pl.BlockSpec((1, D_MODEL), lambda i: (0, 0)), wspec, wspec],
        out_specs=[row, row],
        out_shape=[jax.ShapeDtypeStruct((n, D_MODEL), F32)] * 2,
        compiler_params=_cparams(1),
        name="memkv",
    )(mem2d, g.reshape(1, D_MODEL), wk_b, wv_b)


def _mix_mem_kernel(x_ref, pool_ref, sb_ref, df_ref, wout_ref, gpost0_ref, gpre1_ref, gpost1_ref,
                    wq_ref, mk_ref, mv_ref, wo_ref, o_ref, att_ref):
    mixed = (_dot(pool_ref[0], wout_ref[0:POOL_DIM, :])
             + _dot(sb_ref[0], wout_ref[POOL_DIM:POOL_DIM + SB_DIM, :])
             + _dot(df_ref[0], wout_ref[POOL_DIM + SB_DIM:, :]))
    x1 = x_ref[0] + _rms(mixed, gpost0_ref[...])

    hb = _rms(x1, gpre1_ref[...]).astype(BF16)
    qm = (_dot(hb, wq_ref[...]) * (MEM_HEAD_DIM ** -0.5)).astype(BF16)
    for h in range(MEM_HEADS):
        sl = slice(h * MEM_HEAD_DIM, (h + 1) * MEM_HEAD_DIM)
        s = _dot_nt(qm[:, sl], mk_ref[0, :, sl].astype(BF16))
        e = jnp.exp(s - jnp.max(s, axis=-1, keepdims=True))
        p = e / jnp.sum(e, axis=-1, keepdims=True)
        att_ref[:, sl] = _dot(p.astype(BF16), mv_ref[0, :, sl].astype(BF16)).astype(BF16)
    y = _dot(att_ref[...], wo_ref[...])
    o_ref[0] = x1 + _rms(y, gpost1_ref[...])


def _mix_mem(x, pool_o, sb_o, df_o, wout_b, g_post0, g_pre1, g_post1, wq_b, mk, mv, wo_b, tm):
    b, t, _ = x.shape
    assert t % tm == 0
    m = mk.shape[1]
    tok = lambda w: pl.BlockSpec((1, tm, w), lambda bi, ti: (bi, ti, 0))
    const = lambda shape: pl.BlockSpec(shape, lambda bi, ti: (0,) * len(shape))
    vec = const((1, D_MODEL))
    memspec = pl.BlockSpec((1, m, D_MODEL), lambda bi, ti: (bi, 0, 0))
    return pl.pallas_call(
        _mix_mem_kernel,
        grid=(b, t // tm),
        in_specs=[tok(D_MODEL), tok(POOL_DIM), tok(SB_DIM), tok(DIFF_DIM),
                  const((D_MODEL, D_MODEL)), vec, vec, vec,
                  const((D_MODEL, D_MODEL)), memspec, memspec, const((D_MODEL, D_MODEL))],
        out_specs=tok(D_MODEL),
        out_shape=jax.ShapeDtypeStruct((b, t, D_MODEL), F32),
        scratch_shapes=[pltpu.VMEM((tm, D_MODEL), BF16)],
        compiler_params=_cparams(2),
        name="mix_mem",
    )(x, pool_o, sb_o, df_o, wout_b, g_post0.reshape(1, -1), g_pre1.reshape(1, -1),
      g_post1.reshape(1, -1), wq_b, mk, mv, wo_b)


def _ffn_kernel(x_ref, gpre_ref, gpost_ref, wg_ref, wu_ref, wd_ref, o_ref, *, fc):
    x = x_ref[...]
    hb = _rms(x, gpre_ref[...]).astype(BF16)
    acc = jnp.zeros(x.shape, F32)
    for c in range(D_FF // fc):
        sl = slice(c * fc, (c + 1) * fc)
        gate = _dot(hb, wg_ref[:, sl])
        up = _dot(hb, wu_ref[:, sl])
        act = (gate / (1.0 + jnp.exp(-gate))) * up
        acc = acc + _dot(act.astype(BF16), wd_ref[sl, :])
    o_ref[...] = x + _rms(acc, gpost_ref[...])


def _ffn(x2d, g_pre2, g_post2, wg_b, wu_b, wd_b, tm, fc=256):
    n = x2d.shape[0]
    assert n % tm == 0 and D_FF % fc == 0
    row = pl.BlockSpec((tm, D_MODEL), lambda i: (i, 0))
    vec = pl.BlockSpec((1, D_MODEL), lambda i: (0, 0))
    return pl.pallas_call(
        functools.partial(_ffn_kernel, fc=fc),
        grid=(n // tm,),
        in_specs=[row, vec, vec,
                  pl.BlockSpec((D_MODEL, D_FF), lambda i: (0, 0)),
                  pl.BlockSpec((D_MODEL, D_FF), lambda i: (0, 0)),
                  pl.BlockSpec((D_FF, D_MODEL), lambda i: (0, 0))],
        out_specs=row,
        out_shape=jax.ShapeDtypeStruct((n, D_MODEL), F32),
        compiler_params=_cparams(1),
        name="ffn",
    )(x2d, g_pre2.reshape(1, -1), g_post2.reshape(1, -1), wg_b, wu_b, wd_b)


def _pick(n, pref):
    return pref if n % pref == 0 else n


def _layer(x, pool_buf, sb_k_past, sb_v_past, diff_k_past, diff_v_past, mem_k, mem_v, lw, lam_init):
    (g_pre, g_post, w_in_b, w_out_b, wbd_b, pool_scale, lam_params, diff_g,
     wq_b, wo_b, wg_b, wu_b, wd_b) = lw
    b, t, _ = x.shape
    n = b * t
    past = 0 if sb_k_past is None else sb_k_past.shape[1]

    (u, sk, sv, dk, dv, sqb, skb, svb, dqb, dkb, dvb) = _inproj(
        x.reshape(n, D_MODEL), g_pre[0], w_in_b, _pick(n, 512))
    r3 = lambda a: a.reshape(b, t, a.shape[-1])
    u, sqb, skb, svb, dqb, dkb, dvb = map(r3, (u, sqb, skb, svb, dqb, dkb, dvb))

    buf16 = jnp.concatenate([jnp.zeros((b, 1, POOL_DIM), F32), pool_buf], axis=1)
    pool_o = _pool(u, buf16, wbd_b, pool_scale, _pick(t, 512), past)
    new_pool = jnp.concatenate([pool_buf, u], axis=1)[:, -POOL_STATE:]

    if past:
        sb_tq, df_tq = t, t
        sb_o = _sb_attention(sqb, skb, svb, sb_k_past.reshape(b, past, SB_DIM),
                             sb_v_past.reshape(b, past, SB_DIM), sb_tq, _pick(past, 256), False)
        df_o = _diff_attention(dqb, dkb, dvb, diff_k_past.reshape(b, past, DIFF_DIM),
                               diff_v_past.reshape(b, past, DIFF_DIM), lam_params, diff_g,
                               df_tq, _pick(past, 512), False, past, lam_init)
    else:
        sb_tq = _pick(t, 256)
        sb_o = _sb_attention(sqb, skb, svb, skb, svb, sb_tq, sb_tq, True)
        df_tq = _pick(t, 1024)
        df_o = _diff_attention(dqb, dkb, dvb, dkb, dvb, lam_params, diff_g,
                               df_tq, df_tq, True, 0, lam_init)

    mem_k3 = mem_k.reshape(b, mem_k.shape[1], D_MODEL)
    mem_v3 = mem_v.reshape(b, mem_v.shape[1], D_MODEL)
    x = _mix_mem(x, pool_o, sb_o, df_o, w_out_b, g_post[0], g_pre[1], g_post[1],
                 wq_b, mem_k3, mem_v3, wo_b, _pick(t, 256))
    x = _ffn(x.reshape(n, D_MODEL), g_pre[2], g_post[2], wg_b, wu_b, wd_b,
             _pick(n, 256)).reshape(b, t, D_MODEL)

    sk = sk.reshape(b, t, SB_HEADS, SB_HEAD_DIM)
    sv = sv.reshape(b, t, SB_HEADS, SB_HEAD_DIM)
    dk = dk.reshape(b, t, DIFF_HEADS, DIFF_HEAD_DIM)
    dv = dv.reshape(b, t, DIFF_HEADS, DIFF_HEAD_DIM)
    return x, sk, sv, dk, dv, new_pool


def _block_diag(pool_w):
    g, c, d = pool_w.shape
    out = jnp.zeros((g * c, g * d), pool_w.dtype)
    for i in range(g):
        out = lax.dynamic_update_slice(out, pool_w[i], (i * c, i * d))
    return out


def kernel(x_prompt, x_sample, cache_sb_k, cache_sb_v, cache_diff_k, cache_diff_v, cache_mem_k, cache_mem_v, state_pool, mem_prompt, g_pre, g_post, g_mem, w_in, w_out, pool_w, pool_scale, lam_q1, lam_k1, lam_q2, lam_k2, diff_g, wq_m, wk_m, wv_m, wo_m, w_gate, w_up, w_down):
    depth = w_in.shape[0]
    xp, xs = x_prompt, x_sample
    bp, mem_len = mem_prompt.shape[0], mem_prompt.shape[1]
    p_sbk, p_sbv, p_dk, p_dv, p_pool, p_mk, p_mv = [], [], [], [], [], [], []
    s_sbk, s_sbv, s_dk, s_dv, s_pool = [], [], [], [], []
    for li in range(depth):
        lam_init = 0.8 - 0.6 * math.exp(-0.3 * li)
        cast = lambda w: w[li].astype(BF16)
        lw = (g_pre[li], g_post[li], cast(w_in), cast(w_out), _block_diag(pool_w[li]).astype(BF16),
              pool_scale[li], (lam_q1[li], lam_k1[li], lam_q2[li], lam_k2[li]), diff_g[li],
              cast(wq_m), cast(wo_m), cast(w_gate), cast(w_up), cast(w_down))
        mk2d, mv2d = _memkv(mem_prompt.reshape(bp * mem_len, D_MODEL), g_mem[li],
                            cast(wk_m), cast(wv_m), _pick(bp * mem_len, 256))
        mk = mk2d.reshape(bp, mem_len, MEM_HEADS, MEM_HEAD_DIM)
        mv = mv2d.reshape(bp, mem_len, MEM_HEADS, MEM_HEAD_DIM)
        zero_buf = jnp.zeros((xp.shape[0], POOL_STATE, POOL_DIM), xp.dtype)
        xp, sk, sv, dk, dv, npool = _layer(xp, zero_buf, None, None, None, None, mk, mv, lw, lam_init)
        p_sbk.append(sk); p_sbv.append(sv); p_dk.append(dk); p_dv.append(dv)
        p_pool.append(npool); p_mk.append(mk); p_mv.append(mv)
        xs, sk, sv, dk, dv, npool = _layer(xs, state_pool[li], cache_sb_k[li], cache_sb_v[li],
                                           cache_diff_k[li], cache_diff_v[li],
                                           cache_mem_k[li], cache_mem_v[li], lw, lam_init)
        s_sbk.append(sk); s_sbv.append(sv); s_dk.append(dk); s_dv.append(dv); s_pool.append(npool)
    return (xp, xs,
            jnp.stack(p_sbk), jnp.stack(p_sbv), jnp.stack(p_dk), jnp.stack(p_dv),
            jnp.stack(p_pool), jnp.stack(p_mk), jnp.stack(p_mv),
            jnp.stack(s_sbk), jnp.stack(s_sbv), jnp.stack(s_dk), jnp.stack(s_dv),
            jnp.stack(s_pool))
```

```python
import functools
import math

import jax
import jax.numpy as jnp
from jax import lax
from jax.experimental import pallas as pl
from jax.experimental.pallas import tpu as pltpu

F32 = jnp.float32
BF16 = jnp.bfloat16

D_MODEL = 1024
CHUNK = 64
POOL_WINDOWS = (2, 4, 8, 16)
POOL_GROUP_DIM = 64
POOL_DIM = 256
POOL_STATE = 15
SB_HEADS = 4
SB_HEAD_DIM = 64
SB_DIM = 256
DIFF_HEADS = 4
DIFF_HALF_DIM = 64
DIFF_HEAD_DIM = 128
DIFF_DIM = 512
MEM_HEADS = 4
MEM_HEAD_DIM = 256
D_FF = 2816
EPS = 1e-6
NEG_INF = -1e30
LOG2E = 1.4426950408889634

LANES = 128
MXU_DIM = 256
VMEM_LIMIT_BYTES = 56 * 1024 * 1024

SB_ZERO_LOG = 110.0


def _cparams(n_grid):
    return pltpu.CompilerParams(
        dimension_semantics=("parallel",) * n_grid,
        vmem_limit_bytes=VMEM_LIMIT_BYTES,
    )


def _rms(x, g):
    return x * lax.rsqrt(jnp.mean(x * x, axis=-1, keepdims=True) + EPS) * g


def _dot(a, b):
    return jnp.dot(a, b, preferred_element_type=F32)


def _dot_nt(a, b):
    return lax.dot_general(a, b, (((1,), (1,)), ((), ())), preferred_element_type=F32)


def _inproj_kernel(x_ref, g_ref, w_ref,
                   u_ref, sk_ref, sv_ref, dk_ref, dv_ref,
                   sqb_ref, skb_ref, svb_ref, dqb_ref, dkb_ref, dvb_ref):
    hb = _rms(x_ref[...], g_ref[...]).astype(BF16)

    def cols(lo, hi):
        return _dot(hb, w_ref[:, lo:hi])

    u_ref[...] = cols(0, 256)
    sqb_ref[...] = (cols(256, 512) * (SB_HEAD_DIM ** -0.5)).astype(BF16)
    sk = cols(512, 768)
    sk_ref[...] = sk
    skb_ref[...] = sk.astype(BF16)
    sv = cols(768, 1024)
    sv_ref[...] = sv
    svb_ref[...] = sv.astype(BF16)
    dqb_ref[...] = (cols(1024, 1536) * (LOG2E * DIFF_HALF_DIM ** -0.5)).astype(BF16)
    tm = x_ref.shape[0]
    dk = cols(1536, 2048)
    dkb_ref[...] = dk.astype(BF16)
    dv = cols(2048, 2560)
    dvb_ref[...] = dv.astype(BF16)
    for h in range(DIFF_HEADS):
        sl = slice(h * DIFF_HEAD_DIM, (h + 1) * DIFF_HEAD_DIM)
        dk_ref[pl.ds(h, tm, stride=DIFF_HEADS), :] = dk[:, sl]
        dv_ref[pl.ds(h, tm, stride=DIFF_HEADS), :] = dv[:, sl]


def _inproj(x2d, g, w_in_b, tm):
    n = x2d.shape[0]
    assert n % tm == 0
    widths = (256, 256, 256, 512, 512, 256, 256, 256, 512, 512, 512)
    dtypes = (F32,) * 5 + (BF16,) * 6
    out_shape = [jax.ShapeDtypeStruct((n, w), dt) for w, dt in zip(widths, dtypes)]
    out_specs = [pl.BlockSpec((tm, w), lambda i: (i, 0)) for w in widths]
    for i in (3, 4):
        out_shape[i] = jax.ShapeDtypeStruct((n * DIFF_HEADS, DIFF_HEAD_DIM), F32)
        out_specs[i] = pl.BlockSpec((tm * DIFF_HEADS, DIFF_HEAD_DIM), lambda i: (i, 0))
    return pl.pallas_call(
        _inproj_kernel,
        grid=(n // tm,),
        in_specs=[
            pl.BlockSpec((tm, D_MODEL), lambda i: (i, 0)),
            pl.BlockSpec((1, D_MODEL), lambda i: (0, 0)),
            pl.BlockSpec(w_in_b.shape, lambda i: (0, 0)),
        ],
        out_specs=out_specs,
        out_shape=out_shape,
        compiler_params=_cparams(1),
        name="inproj",
    )(x2d, g.reshape(1, D_MODEL), w_in_b)


def _pool_kernel(u_ref, prev_ref, buf_ref, wbd_ref, scale_ref, o_ref, xs_ref, *, tm, start_pos):
    t = pl.program_id(1)
    u = u_ref[0]
    xs_ref[0:16, :] = jnp.where(t == 0, buf_ref[0], prev_ref[0])
    xs_ref[16:, :] = u
    pos = start_pos + t * tm + lax.broadcasted_iota(jnp.int32, (tm, POOL_DIM), 0)
    group = lax.broadcasted_iota(jnp.int32, (tm, POOL_DIM), 1) // POOL_GROUP_DIM
    mean = jnp.zeros((tm, POOL_DIM), F32)
    run = u
    gi = 0
    for back in range(1, max(POOL_WINDOWS)):
        run = run + xs_ref[16 - back:16 - back + tm, :]
        if back + 1 == POOL_WINDOWS[gi]:
            cnt = jnp.minimum(back + 1, pos + 1).astype(F32)
            mean = jnp.where(group == gi, run / cnt, mean)
            gi += 1
    d = (mean - u).astype(BF16)
    o_ref[0] = (_dot(d, wbd_ref[...]) * scale_ref[...]).astype(BF16)


def _pool(u, buf16, wbd_b, pool_scale, tm, start_pos):
    b, t, _ = u.shape
    assert t % tm == 0 and tm % 16 == 0
    r = tm // 16
    return pl.pallas_call(
        functools.partial(_pool_kernel, tm=tm, start_pos=start_pos),
        grid=(b, t // tm),
        in_specs=[
            pl.BlockSpec((1, tm, POOL_DIM), lambda bi, ti: (bi, ti, 0)),
            pl.BlockSpec((1, 16, POOL_DIM), lambda bi, ti: (bi, jnp.maximum(ti * r - 1, 0), 0)),
            pl.BlockSpec((1, 16, POOL_DIM), lambda bi, ti: (bi, 0, 0)),
            pl.BlockSpec((POOL_DIM, POOL_DIM), lambda bi, ti: (0, 0)),
            pl.BlockSpec((1, POOL_DIM), lambda bi, ti: (0, 0)),
        ],
        out_specs=pl.BlockSpec((1, tm, POOL_DIM), lambda bi, ti: (bi, ti, 0)),
        out_shape=jax.ShapeDtypeStruct((b, t, POOL_DIM), BF16),
        scratch_shapes=[pltpu.VMEM((tm + 16, POOL_DIM), F32)],
        compiler_params=_cparams(2),
        name="pool",
    )(u, u, buf16, wbd_b, pool_scale.reshape(1, POOL_DIM))


def _suffix_ones(n):
    r = lax.broadcasted_iota(jnp.int32, (n, n), 0)
    c = lax.broadcasted_iota(jnp.int32, (n, n), 1)
    return jnp.where(r >= c, 1.0, 0.0).astype(BF16)


def _sb_kernel(q_ref, kd_ref, vd_ref, kp_ref, vp_ref, o_ref, acc_ref, ca_ref, cb_ref,
               *, tq, tk, past_blocks):
    qi = pl.program_id(2)
    lane = lax.broadcasted_iota(jnp.int32, (tq, LANES), 1)
    first = lane < SB_HEAD_DIM
    qf = q_ref[0].astype(F32)
    qa = jnp.where(first, qf, 0.0).astype(BF16)
    qb = jnp.where(first, 0.0, qf).astype(BF16)

    def head(qh, k, mask, ones):
        z = _dot_nt(qh, k)
        sp = jnp.maximum(z, 0.0) + jnp.log1p(jnp.exp(-jnp.abs(z)))
        if mask is not None:
            sp = jnp.where(mask, sp, 0.0)
        hi = sp.astype(BF16)
        lo = (sp - hi.astype(F32)).astype(BF16)
        return z, _dot(hi, ones) + _dot(lo, ones)

    def block(k, v, mask, ones):
        ca = ca_ref[...]
        cb = cb_ref[...]
        za, csa = head(qa, k, mask, ones)
        zb, csb = head(qb, k, mask, ones)
        wa = jnp.exp(za - csa - ca)
        wb = jnp.exp(zb - csb - cb)
        if mask is not None:
            wa = jnp.where(mask, wa, 0.0)
            wb = jnp.where(mask, wb, 0.0)
        pva = _dot(wa.astype(BF16), v)
        pvb = _dot(wb.astype(BF16), v)
        acc_ref[...] += jnp.where(first, pva, pvb)
        ca = ca + csa[:, 0:1]
        cb = cb + csb[:, 0:1]
        ca_ref[...] = ca
        cb_ref[...] = cb
        return jnp.min(jnp.minimum(ca, cb))

    acc_ref[...] = jnp.zeros_like(acc_ref)
    ca_ref[...] = jnp.zeros_like(ca_ref)
    cb_ref[...] = jnp.zeros_like(cb_ref)

    row = lax.broadcasted_iota(jnp.int32, (tq, tq), 0)
    col = lax.broadcasted_iota(jnp.int32, (tq, tq), 1)
    cmin = block(kd_ref[0], vd_ref[0], col < row, _suffix_ones(tq))

    ones_p = _suffix_ones(tk)

    def cond(state):
        j, cm = state
        return jnp.logical_and(j >= 0, cm <= SB_ZERO_LOG)

    def body(state):
        j, _ = state
        start = pl.multiple_of(j * tk, tk)
        k = kp_ref[0, pl.ds(start, tk), :].astype(BF16)
        v = vp_ref[0, pl.ds(start, tk), :].astype(BF16)
        return j - 1, block(k, v, None, ones_p)

    lax.while_loop(cond, body, (past_blocks(qi) - 1, cmin))
    o_ref[0] = acc_ref[...].astype(BF16)


def _sb_attention(qb, kb, vb, k_past, v_past, tq, tk, from_self):
    b, t, _ = qb.shape
    p = k_past.shape[1]
    assert t % tq == 0 and p % tk == 0
    if from_self:
        assert tq % tk == 0
        past_blocks = lambda qi: qi * (tq // tk)
    else:
        past_blocks = lambda qi: jnp.int32(p // tk)
    tile = pl.BlockSpec((1, tq, LANES), lambda bi, hi, qi: (bi, qi, hi))
    full = pl.BlockSpec((1, p, LANES), lambda bi, hi, qi: (bi, 0, hi))
    return pl.pallas_call(
        functools.partial(_sb_kernel, tq=tq, tk=tk, past_blocks=past_blocks),
        grid=(b, SB_DIM // LANES, t // tq),
        in_specs=[tile, tile, tile, full, full],
        out_specs=tile,
        out_shape=jax.ShapeDtypeStruct((b, t, SB_DIM), BF16),
        scratch_shapes=[
            pltpu.VMEM((tq, LANES), F32),
            pltpu.VMEM((tq, 1), F32),
            pltpu.VMEM((tq, 1), F32),
        ],
        compiler_params=_cparams(3),
        name="sb_attn",
    )(qb, kb, vb, k_past, v_past)


def _exp2_rows(s, m):
    n = s.shape[1] // LANES
    if n == 0:
        return jnp.exp2(s - m[:, :s.shape[1]]).astype(BF16)
    return jnp.concatenate(
        [jnp.exp2(s[:, c * LANES:(c + 1) * LANES] - m).astype(BF16) for c in range(n)], axis=1)


def _diff_kernel(*refs, tq, tk, row_chunk, head_stride, diag_in_past, past_blocks, q_pos0,
                 lam_init):
    if diag_in_past:
        q_ref, kp_ref, vp_ref = refs[:3]
        kd_ref = vd_ref = None
        rest = refs[3:]
    else:
        q_ref, kd_ref, vd_ref, kp_ref, vp_ref = refs[:5]
        rest = refs[5:]
    (lq1_ref, lk1_ref, lq2_ref, lk2_ref, g_ref, o_ref, m_ref, a_ref, p_ref, alpha_ref) = rest
    qi = pl.program_id(2)
    lane = lax.broadcasted_iota(jnp.int32, (tq, LANES), 1)
    first = lane < DIFF_HALF_DIM
    qf = q_ref[0].astype(F32)
    qs = (jnp.where(first, qf, 0.0).astype(BF16), jnp.where(first, 0.0, qf).astype(BF16))
    n_max = kp_ref.shape[0] // (tk * head_stride)
    rows = min(tq, row_chunk)

    def with_ones(v):
        return jnp.concatenate([v, jnp.ones(v.shape, BF16)], axis=1)

    def past(ref, j):
        start = jnp.clip(j, 0, n_max - 1) * tk
        if head_stride == 1:
            return ref[pl.ds(pl.multiple_of(start, tk), tk), :].astype(BF16)
        first_row = start * head_stride + pl.program_id(1)
        return ref[pl.ds(first_row, tk, stride=head_stride), :].astype(BF16)

    def own_mask(r, n_rows, n_keys):
        qpos = q_pos0(qi) + r + lax.broadcasted_iota(jnp.int32, (n_rows, n_keys), 0)
        kpos = q_pos0(qi) + lax.broadcasted_iota(jnp.int32, (n_rows, n_keys), 1)
        return (kpos // CHUNK) <= (qpos // CHUNK)

    def step(k, v_pending, own, pending_own=False):
        v1 = None if v_pending is None else with_ones(v_pending)
        for r in range(0, tq, rows):
            sl = slice(r, r + rows)
            for h in range(2):
                if v1 is not None:
                    nk = r + rows if pending_own else tk
                    alpha = alpha_ref[h, sl]
                    a_ref[h, sl] = (jnp.concatenate([alpha, alpha], axis=1) * a_ref[h, sl]
                                    + _dot(p_ref[h, sl, :nk], v1[:nk]))
                if k is not None:
                    nk = r + rows if own else tk
                    s = _dot_nt(qs[h][sl], k[:nk])
                    if own:
                        s = jnp.where(own_mask(r, rows, nk), s, NEG_INF)
                    m_old = m_ref[h, sl]
                    m_new = jnp.maximum(m_old, jnp.max(s, axis=-1, keepdims=True))
                    alpha_ref[h, sl] = jnp.exp2(m_old - m_new)
                    m_ref[h, sl] = m_new
                    p_ref[h, sl, :nk] = _exp2_rows(s, m_new)

    n_past = past_blocks(qi)
    a_ref[...] = jnp.zeros_like(a_ref)
    if diag_in_past:
        m_ref[...] = jnp.full(m_ref.shape, NEG_INF, F32)
    else:
        for h in range(2):
            s = jnp.where(own_mask(0, tq, tq), _dot_nt(qs[h], kd_ref[0]), NEG_INF)
            m = jnp.broadcast_to(jnp.max(s, axis=-1, keepdims=True), (tq, LANES))
            m_ref[h] = m
            a_ref[h] = _dot(_exp2_rows(s, m), with_ones(vd_ref[0]))
    p_ref[...] = jnp.zeros_like(p_ref)
    alpha_ref[...] = jnp.ones_like(alpha_ref)

    def body(j, carry):
        step(past(kp_ref, j), past(vp_ref, j - 1), False)
        return carry

    lax.fori_loop(0, n_past, body, 0)
    if diag_in_past:
        step(past(kp_ref, n_past), past(vp_ref, n_past - 1), True)
        step(None, past(vp_ref, n_past), False, pending_own=True)
    else:
        step(None, past(vp_ref, n_past - 1), False)

    a1_ref, a2_ref = a_ref.at[0], a_ref.at[1]
    lam = (jnp.exp(jnp.sum(lq1_ref[...] * lk1_ref[...], axis=-1, keepdims=True))
           - jnp.exp(jnp.sum(lq2_ref[...] * lk2_ref[...], axis=-1, keepdims=True)) + lam_init)
    o = (a1_ref[:, :LANES] / a1_ref[:, LANES:]
         - lam * (a2_ref[:, :LANES] / a2_ref[:, LANES:]))
    o_ref[0] = (_rms(o, g_ref[...]) * (1.0 - lam_init)).astype(BF16)


def _diff_attention(qb, kb, vb, cache, lam_params, diff_g, tq, tk, lam_init):
    b, t, _ = qb.shape
    tile = pl.BlockSpec((1, tq, LANES), lambda bi, hi, qi: (bi, qi, hi))
    vec64 = pl.BlockSpec((1, DIFF_HALF_DIM), lambda bi, hi, qi: (0, 0))
    vec128 = pl.BlockSpec((1, DIFF_HEAD_DIM), lambda bi, hi, qi: (0, 0))
    if cache is None:
        assert tq == tk and t % tq == 0
        past_len, head_stride = 0, 1
        past_blocks = lambda qi: qi
        full = pl.BlockSpec((None, t, LANES), lambda bi, hi, qi: (bi, 0, hi))
        kv_specs, kv_args = [full, full], (kb, vb)
    else:
        k_cache, v_cache, li = cache
        rows = k_cache.shape[2]
        past_len, head_stride = rows // DIFF_HEADS, DIFF_HEADS
        assert t == tq and past_len % tk == 0
        past_blocks = lambda qi: past_len // tk
        full = pl.BlockSpec((None, None, rows, LANES), lambda bi, hi, qi: (li, bi, 0, 0))
        kv_specs, kv_args = [tile, tile, full, full], (kb, vb, k_cache, v_cache)
    q_pos0 = lambda qi: past_len + qi * tq
    return pl.pallas_call(
        functools.partial(_diff_kernel, tq=tq, tk=tk, row_chunk=128, head_stride=head_stride,
                          diag_in_past=cache is None, past_blocks=past_blocks, q_pos0=q_pos0,
                          lam_init=lam_init),
        grid=(b, DIFF_HEADS, t // tq),
        in_specs=[tile] + kv_specs + [vec64, vec64, vec64, vec64, vec128],
        out_specs=tile,
        out_shape=jax.ShapeDtypeStruct((b, t, DIFF_DIM), BF16),
        scratch_shapes=[
            pltpu.VMEM((2, tq, LANES), F32), pltpu.VMEM((2, tq, 2 * LANES), F32),
            pltpu.VMEM((2, tq, tk), BF16), pltpu.VMEM((2, tq, LANES), F32),
        ],
        compiler_params=_cparams(3),
        name="diff_attn",
    )(qb, *kv_args,
      *[a.reshape(1, DIFF_HALF_DIM) for a in lam_params], diff_g.reshape(1, DIFF_HEAD_DIM))


def _memkv_kernel(m_ref, g_ref, wk_ref, wv_ref, k_ref, v_ref):
    hb = _rms(m_ref[...], g_ref[...]).astype(BF16)
    k_ref[...] = _dot(hb, wk_ref[...])
    v_ref[...] = _dot(hb, wv_ref[...])


def _memkv(mem2d, g, wk_b, wv_b, tm):
    n = mem2d.shape[0]
    assert n % tm == 0
    row = pl.BlockSpec((tm, D_MODEL), lambda i: (i, 0))
    wspec = pl.BlockSpec((D_MODEL, D_MODEL), lambda i: (0, 0))
    return pl.pallas_call(
        _memkv_kernel,
        grid=(n // tm,),
        in_specs=[row, pl.BlockSpec((1, D_MODEL), lambda i: (0, 0)), wspec, wspec],
        out_specs=[row, row],
        out_shape=[jax.ShapeDtypeStruct((n, D_MODEL), F32)] * 2,
        compiler_params=_cparams(1),
        name="memkv",
    )(mem2d, g.reshape(1, D_MODEL), wk_b, wv_b)


def _mix_mem_kernel(x_ref, pool_ref, sb_ref, df_ref, wout_ref, gpost0_ref, gpre1_ref, gpost1_ref,
                    wq_ref, mk_ref, mv_ref, wo_ref, o_ref, att_ref, *, sub):
    for r in range(0, x_ref.shape[1], sub):
        rows = slice(r, r + sub)
        mixed = (_dot(pool_ref[0, rows, :], wout_ref[0:POOL_DIM, :])
                 + _dot(sb_ref[0, rows, :], wout_ref[POOL_DIM:POOL_DIM + SB_DIM, :])
                 + _dot(df_ref[0, rows, :], wout_ref[POOL_DIM + SB_DIM:, :]))
        x1 = x_ref[0, rows, :] + _rms(mixed, gpost0_ref[...])

        hb = _rms(x1, gpre1_ref[...]).astype(BF16)
        qm = (_dot(hb, wq_ref[...]) * (MEM_HEAD_DIM ** -0.5)).astype(BF16)
        for h in range(MEM_HEADS):
            sl = slice(h * MEM_HEAD_DIM, (h + 1) * MEM_HEAD_DIM)
            s = _dot_nt(qm[:, sl], mk_ref[0, :, sl].astype(BF16))
            e = jnp.exp(s - jnp.max(s, axis=-1, keepdims=True))
            p = e / jnp.sum(e, axis=-1, keepdims=True)
            att_ref[rows, sl] = _dot(p.astype(BF16), mv_ref[0, :, sl].astype(BF16)).astype(BF16)
        y = _dot(att_ref[rows, :], wo_ref[...])
        o_ref[0, rows, :] = x1 + _rms(y, gpost1_ref[...])


def _mix_mem(x, pool_o, sb_o, df_o, wout_b, g_post0, g_pre1, g_post1, wq_b, mk, mv, wo_b, tm):
    b, t, _ = x.shape
    assert t % tm == 0
    m = mk.shape[1]
    tok = lambda w: pl.BlockSpec((1, tm, w), lambda bi, ti: (bi, ti, 0))
    const = lambda shape: pl.BlockSpec(shape, lambda bi, ti: (0,) * len(shape))
    vec = const((1, D_MODEL))
    memspec = pl.BlockSpec((1, m, D_MODEL), lambda bi, ti: (bi, 0, 0))
    return pl.pallas_call(
        functools.partial(_mix_mem_kernel, sub=min(tm, 256)),
        grid=(b, t // tm),
        in_specs=[tok(D_MODEL), tok(POOL_DIM), tok(SB_DIM), tok(DIFF_DIM),
                  const((D_MODEL, D_MODEL)), vec, vec, vec,
                  const((D_MODEL, D_MODEL)), memspec, memspec, const((D_MODEL, D_MODEL))],
        out_specs=tok(D_MODEL),
        out_shape=jax.ShapeDtypeStruct((b, t, D_MODEL), F32),
        scratch_shapes=[pltpu.VMEM((tm, D_MODEL), BF16)],
        compiler_params=_cparams(2),
        name="mix_mem",
    )(x, pool_o, sb_o, df_o, wout_b, g_post0.reshape(1, -1), g_pre1.reshape(1, -1),
      g_post1.reshape(1, -1), wq_b, mk, mv, wo_b)


def _ffn_kernel(x_ref, gpre_ref, gpost_ref, wg_ref, wu_ref, wd_ref, o_ref, *, fc, sub):
    for r in range(0, x_ref.shape[0], sub):
        x = x_ref[r:r + sub, :]
        hb = _rms(x, gpre_ref[...]).astype(BF16)
        acc = jnp.zeros(x.shape, F32)
        for c in range(D_FF // fc):
            sl = slice(c * fc, (c + 1) * fc)
            gate = _dot(hb, wg_ref[:, sl])
            up = _dot(hb, wu_ref[:, sl])
            act = (gate / (1.0 + jnp.exp(-gate))) * up
            acc = acc + _dot(act.astype(BF16), wd_ref[sl, :])
        o_ref[r:r + sub, :] = x + _rms(acc, gpost_ref[...])


def _ffn(x2d, g_pre2, g_post2, wg_b, wu_b, wd_b, tm, fc=256):
    n = x2d.shape[0]
    assert n % tm == 0 and D_FF % fc == 0
    row = pl.BlockSpec((tm, D_MODEL), lambda i: (i, 0))
    vec = pl.BlockSpec((1, D_MODEL), lambda i: (0, 0))
    return pl.pallas_call(
        functools.partial(_ffn_kernel, fc=fc, sub=min(tm, 256)),
        grid=(n // tm,),
        in_specs=[row, vec, vec,
                  pl.BlockSpec((D_MODEL, D_FF), lambda i: (0, 0)),
                  pl.BlockSpec((D_MODEL, D_FF), lambda i: (0, 0)),
                  pl.BlockSpec((D_FF, D_MODEL), lambda i: (0, 0))],
        out_specs=row,
        out_shape=jax.ShapeDtypeStruct((n, D_MODEL), F32),
        compiler_params=_cparams(1),
        name="ffn",
    )(x2d, g_pre2.reshape(1, -1), g_post2.reshape(1, -1), wg_b, wu_b, wd_b)


def _pick(n, pref):
    return pref if n % pref == 0 else n


def _layer(x, pool_buf, sb_k_past, sb_v_past, diff_cache, mem_k, mem_v, lw, lam_init):
    (g_pre, g_post, w_in_b, w_out_b, wbd_b, pool_scale, lam_params, diff_g,
     wq_b, wo_b, wg_b, wu_b, wd_b) = lw
    b, t, _ = x.shape
    n = b * t
    past = 0 if sb_k_past is None else sb_k_past.shape[1]

    (u, sk, sv, dk, dv, sqb, skb, svb, dqb, dkb, dvb) = _inproj(
        x.reshape(n, D_MODEL), g_pre[0], w_in_b, _pick(n, 512))
    r3 = lambda a: a.reshape(b, t, a.shape[-1])
    u, sqb, skb, svb, dqb, dkb, dvb = map(r3, (u, sqb, skb, svb, dqb, dkb, dvb))

    buf16 = jnp.concatenate([jnp.zeros((b, 1, POOL_DIM), F32), pool_buf], axis=1)
    pool_o = _pool(u, buf16, wbd_b, pool_scale, _pick(t, 512), past)
    new_pool = jnp.concatenate([pool_buf, u], axis=1)[:, -POOL_STATE:]

    if past:
        sb_tq, df_tq = t, t
        sb_o = _sb_attention(sqb, skb, svb, sb_k_past.reshape(b, past, SB_DIM),
                             sb_v_past.reshape(b, past, SB_DIM), sb_tq, _pick(past, 256), False)
        df_o = _diff_attention(dqb, dkb, dvb, diff_cache, lam_params, diff_g,
                               df_tq, _pick(past, 512), lam_init)
    else:
        sb_tq = _pick(t, 256)
        sb_o = _sb_attention(sqb, skb, svb, skb, svb, sb_tq, sb_tq, True)
        df_tq = _pick(t, 1024)
        df_o = _diff_attention(dqb, dkb, dvb, None, lam_params, diff_g, df_tq, df_tq, lam_init)

    mem_k3 = mem_k.reshape(b, mem_k.shape[1], D_MODEL)
    mem_v3 = mem_v.reshape(b, mem_v.shape[1], D_MODEL)
    x = _mix_mem(x, pool_o, sb_o, df_o, w_out_b, g_post[0], g_pre[1], g_post[1],
                 wq_b, mem_k3, mem_v3, wo_b, _pick(t, 512))
    x = _ffn(x.reshape(n, D_MODEL), g_pre[2], g_post[2], wg_b, wu_b, wd_b,
             _pick(n, 512)).reshape(b, t, D_MODEL)

    sk = sk.reshape(b, t, SB_HEADS, SB_HEAD_DIM)
    sv = sv.reshape(b, t, SB_HEADS, SB_HEAD_DIM)
    dk = dk.reshape(b, t, DIFF_HEADS, DIFF_HEAD_DIM)
    dv = dv.reshape(b, t, DIFF_HEADS, DIFF_HEAD_DIM)
    return x, sk, sv, dk, dv, new_pool


def _block_diag(pool_w):
    g, c, d = pool_w.shape
    out = jnp.zeros((g * c, g * d), pool_w.dtype)
    for i in range(g):
        out = lax.dynamic_update_slice(out, pool_w[i], (i * c, i * d))
    return out


def kernel(x_prompt, x_sample, cache_sb_k, cache_sb_v, cache_diff_k, cache_diff_v, cache_mem_k, cache_mem_v, state_pool, mem_prompt, g_pre, g_post, g_mem, w_in, w_out, pool_w, pool_scale, lam_q1, lam_k1, lam_q2, lam_k2, diff_g, wq_m, wk_m, wv_m, wo_m, w_gate, w_up, w_down):
    depth = w_in.shape[0]
    xp, xs = x_prompt, x_sample
    bp, mem_len = mem_prompt.shape[0], mem_prompt.shape[1]
    dshape = cache_diff_k.shape
    diff_k_rows = cache_diff_k.reshape(dshape[0], dshape[1], dshape[2] * dshape[3], dshape[4])
    diff_v_rows = cache_diff_v.reshape(dshape[0], dshape[1], dshape[2] * dshape[3], dshape[4])
    p_sbk, p_sbv, p_dk, p_dv, p_pool, p_mk, p_mv = [], [], [], [], [], [], []
    s_sbk, s_sbv, s_dk, s_dv, s_pool = [], [], [], [], []
    for li in range(depth):
        lam_init = 0.8 - 0.6 * math.exp(-0.3 * li)
        cast = lambda w: w[li].astype(BF16)
        lw = (g_pre[li], g_post[li], cast(w_in), cast(w_out), _block_diag(pool_w[li]).astype(BF16),
              pool_scale[li], (lam_q1[li], lam_k1[li], lam_q2[li], lam_k2[li]), diff_g[li],
              cast(wq_m), cast(wo_m), cast(w_gate), cast(w_up), cast(w_down))
        mk2d, mv2d = _memkv(mem_prompt.reshape(bp * mem_len, D_MODEL), g_mem[li],
                            cast(wk_m), cast(wv_m), _pick(bp * mem_len, 256))
        mk = mk2d.reshape(bp, mem_len, MEM_HEADS, MEM_HEAD_DIM)
        mv = mv2d.reshape(bp, mem_len, MEM_HEADS, MEM_HEAD_DIM)
        zero_buf = jnp.zeros((xp.shape[0], POOL_STATE, POOL_DIM), xp.dtype)
        xp, sk, sv, dk, dv, npool = _layer(xp, zero_buf, None, None, None, mk, mv, lw, lam_init)
        p_sbk.append(sk); p_sbv.append(sv); p_dk.append(dk); p_dv.append(dv)
        p_pool.append(npool); p_mk.append(mk); p_mv.append(mv)
        xs, sk, sv, dk, dv, npool = _layer(xs, state_pool[li], cache_sb_k[li], cache_sb_v[li],
                                           (diff_k_rows, diff_v_rows, li),
                                           cache_mem_k[li], cache_mem_v[li], lw, lam_init)
        s_sbk.append(sk); s_sbv.append(sv); s_dk.append(dk); s_dv.append(dv); s_pool.append(npool)
    return (xp, xs,
            jnp.stack(p_sbk), jnp.stack(p_sbv), jnp.stack(p_dk), jnp.stack(p_dv),
            jnp.stack(p_pool), jnp.stack(p_mk), jnp.stack(p_mv),
            jnp.stack(s_sbk), jnp.stack(s_sbv), jnp.stack(s_dk), jnp.stack(s_dv),
            jnp.stack(s_pool))
```

```python
import functools
import math

import jax
import jax.numpy as jnp
from jax import lax
from jax.experimental import pallas as pl
from jax.experimental.pallas import tpu as pltpu

F32 = jnp.float32
BF16 = jnp.bfloat16

D_MODEL = 1024
CHUNK = 64
POOL_WINDOWS = (2, 4, 8, 16)
POOL_GROUP_DIM = 64
POOL_DIM = 256
POOL_STATE = 15
SB_HEADS = 4
SB_HEAD_DIM = 64
SB_DIM = 256
DIFF_HEADS = 4
DIFF_HALF_DIM = 64
DIFF_HEAD_DIM = 128
DIFF_DIM = 512
MEM_HEADS = 4
MEM_HEAD_DIM = 256
D_FF = 2816
EPS = 1e-6
NEG_INF = -1e30
LOG2E = 1.4426950408889634

LANES = 128
MXU_DIM = 256
VMEM_LIMIT_BYTES = 56 * 1024 * 1024

SB_ZERO_LOG = 110.0


def _cparams(n_grid):
    return pltpu.CompilerParams(
        dimension_semantics=("parallel",) * n_grid,
        vmem_limit_bytes=VMEM_LIMIT_BYTES,
    )


def _rms(x, g):
    return x * lax.rsqrt(jnp.mean(x * x, axis=-1, keepdims=True) + EPS) * g


def _dot(a, b):
    return jnp.dot(a, b, preferred_element_type=F32)


def _dot_nt(a, b):
    return lax.dot_general(a, b, (((1,), (1,)), ((), ())), preferred_element_type=F32)


def _inproj_kernel(x_ref, g_ref, w_ref,
                   u_ref, sk_ref, sv_ref, dk_ref, dv_ref,
                   sqb_ref, skb_ref, svb_ref, dqb_ref, dkb_ref, dvb_ref, *, transpose_kv):
    hb = _rms(x_ref[...], g_ref[...]).astype(BF16)

    def cols(lo, hi):
        return _dot(hb, w_ref[:, lo:hi])

    u_ref[...] = cols(0, 256)
    sqb_ref[...] = (cols(256, 512) * (SB_HEAD_DIM ** -0.5)).astype(BF16)
    sk = cols(512, 768)
    skb_ref[...] = sk.astype(BF16)
    sv = cols(768, 1024)
    svb_ref[...] = sv.astype(BF16)
    if transpose_kv:
        sk_ref[0] = sk.T
        sv_ref[0] = sv.T
    else:
        sk_ref[...] = sk
        sv_ref[...] = sv
    dqb_ref[...] = (cols(1024, 1536) * (LOG2E * DIFF_HALF_DIM ** -0.5)).astype(BF16)
    tm = x_ref.shape[0]
    dk = cols(1536, 2048)
    dkb_ref[...] = dk.astype(BF16)
    dv = cols(2048, 2560)
    dvb_ref[...] = dv.astype(BF16)
    for h in range(DIFF_HEADS):
        sl = slice(h * DIFF_HEAD_DIM, (h + 1) * DIFF_HEAD_DIM)
        dk_ref[pl.ds(h, tm, stride=DIFF_HEADS), :] = dk[:, sl]
        dv_ref[pl.ds(h, tm, stride=DIFF_HEADS), :] = dv[:, sl]


def _layer_weight(w, li):
    return pl.BlockSpec((None,) + w.shape[1:], lambda *_: (li, 0, 0),
                        pipeline_mode=pl.Buffered(1))


def _inproj(x, g, w_in_b, li, tm):
    b, t, _ = x.shape
    n = b * t
    assert n % tm == 0
    transpose_kv = t % tm == 0 and tm % LANES == 0
    widths = (256, 256, 256, 512, 512, 256, 256, 256, 512, 512, 512)
    dtypes = (F32,) * 5 + (BF16,) * 6
    out_shape = [jax.ShapeDtypeStruct((n, w), dt) for w, dt in zip(widths, dtypes)]
    out_specs = [pl.BlockSpec((tm, w), lambda i: (i, 0)) for w in widths]
    for i in (3, 4):
        out_shape[i] = jax.ShapeDtypeStruct((n * DIFF_HEADS, DIFF_HEAD_DIM), F32)
        out_specs[i] = pl.BlockSpec((tm * DIFF_HEADS, DIFF_HEAD_DIM), lambda i: (i, 0))
    if transpose_kv:
        per_batch = t // tm
        for i in (1, 2):
            out_shape[i] = jax.ShapeDtypeStruct((b, SB_DIM, t), F32)
            out_specs[i] = pl.BlockSpec((1, SB_DIM, tm),
                                        lambda i: (i // per_batch, 0, i % per_batch))
    return pl.pallas_call(
        functools.partial(_inproj_kernel, transpose_kv=transpose_kv),
        grid=(n // tm,),
        in_specs=[
            pl.BlockSpec((tm, D_MODEL), lambda i: (i, 0)),
            pl.BlockSpec((1, D_MODEL), lambda i: (0, 0)),
            _layer_weight(w_in_b, li),
        ],
        out_specs=out_specs,
        out_shape=out_shape,
        compiler_params=_cparams(1),
        name="inproj",
    )(x.reshape(n, D_MODEL), g.reshape(1, D_MODEL), w_in_b)


def _pool_kernel(u_ref, prev_ref, buf_ref, wbd_ref, scale_ref, o_ref, xs_ref, *, tm, start_pos):
    t = pl.program_id(1)
    u = u_ref[0]
    xs_ref[0:16, :] = jnp.where(t == 0, buf_ref[0], prev_ref[0])
    xs_ref[16:, :] = u
    pos = start_pos + t * tm + lax.broadcasted_iota(jnp.int32, (tm, POOL_DIM), 0)
    group = lax.broadcasted_iota(jnp.int32, (tm, POOL_DIM), 1) // POOL_GROUP_DIM
    mean = jnp.zeros((tm, POOL_DIM), F32)
    run = u
    gi = 0
    for back in range(1, max(POOL_WINDOWS)):
        run = run + xs_ref[16 - back:16 - back + tm, :]
        if back + 1 == POOL_WINDOWS[gi]:
            cnt = jnp.minimum(back + 1, pos + 1).astype(F32)
            mean = jnp.where(group == gi, run / cnt, mean)
            gi += 1
    d = (mean - u).astype(BF16)
    o_ref[0] = (_dot(d, wbd_ref[...]) * scale_ref[...]).astype(BF16)


def _pool(u, buf16, wbd_b, pool_scale, tm, start_pos):
    b, t, _ = u.shape
    assert t % tm == 0 and tm % 16 == 0
    r = tm // 16
    return pl.pallas_call(
        functools.partial(_pool_kernel, tm=tm, start_pos=start_pos),
        grid=(b, t // tm),
        in_specs=[
            pl.BlockSpec((1, tm, POOL_DIM), lambda bi, ti: (bi, ti, 0)),
            pl.BlockSpec((1, 16, POOL_DIM), lambda bi, ti: (bi, jnp.maximum(ti * r - 1, 0), 0)),
            pl.BlockSpec((1, 16, POOL_DIM), lambda bi, ti: (bi, 0, 0)),
            pl.BlockSpec((POOL_DIM, POOL_DIM), lambda bi, ti: (0, 0)),
            pl.BlockSpec((1, POOL_DIM), lambda bi, ti: (0, 0)),
        ],
        out_specs=pl.BlockSpec((1, tm, POOL_DIM), lambda bi, ti: (bi, ti, 0)),
        out_shape=jax.ShapeDtypeStruct((b, t, POOL_DIM), BF16),
        scratch_shapes=[pltpu.VMEM((tm + 16, POOL_DIM), F32)],
        compiler_params=_cparams(2),
        name="pool",
    )(u, u, buf16, wbd_b, pool_scale.reshape(1, POOL_DIM))


def _suffix_ones(n):
    r = lax.broadcasted_iota(jnp.int32, (n, n), 0)
    c = lax.broadcasted_iota(jnp.int32, (n, n), 1)
    return jnp.where(r >= c, 1.0, 0.0).astype(BF16)


def _sb_kernel(q_ref, kd_ref, vd_ref, kp_ref, vp_ref, o_ref, acc_ref, ca_ref, cb_ref,
               *, tq, tk, past_blocks, past_transposed):
    qi = pl.program_id(2)
    lane = lax.broadcasted_iota(jnp.int32, (tq, LANES), 1)
    first = lane < SB_HEAD_DIM
    qf = q_ref[0].astype(F32)
    qa = jnp.where(first, qf, 0.0).astype(BF16)
    qb = jnp.where(first, 0.0, qf).astype(BF16)

    def head(qh, k, mask, ones, transposed):
        z = _dot(qh, k) if transposed else _dot_nt(qh, k)
        sp = jnp.maximum(z, 0.0) + jnp.log1p(jnp.exp(-jnp.abs(z)))
        if mask is not None:
            sp = jnp.where(mask, sp, 0.0)
        hi = sp.astype(BF16)
        lo = (sp - hi.astype(F32)).astype(BF16)
        return z, _dot(hi, ones) + _dot(lo, ones)

    def block(k, v, mask, ones, transposed=False):
        ca = ca_ref[...]
        cb = cb_ref[...]
        za, csa = head(qa, k, mask, ones, transposed)
        zb, csb = head(qb, k, mask, ones, transposed)
        wa = jnp.exp(za - csa - ca)
        wb = jnp.exp(zb - csb - cb)
        if mask is not None:
            wa = jnp.where(mask, wa, 0.0)
            wb = jnp.where(mask, wb, 0.0)
        apply = _dot_nt if transposed else _dot
        pva = apply(wa.astype(BF16), v)
        pvb = apply(wb.astype(BF16), v)
        acc_ref[...] += jnp.where(first, pva, pvb)
        ca = ca + csa[:, 0:1]
        cb = cb + csb[:, 0:1]
        ca_ref[...] = ca
        cb_ref[...] = cb
        return jnp.min(jnp.minimum(ca, cb))

    acc_ref[...] = jnp.zeros_like(acc_ref)
    ca_ref[...] = jnp.zeros_like(ca_ref)
    cb_ref[...] = jnp.zeros_like(cb_ref)

    row = lax.broadcasted_iota(jnp.int32, (tq, tq), 0)
    col = lax.broadcasted_iota(jnp.int32, (tq, tq), 1)
    cmin = block(kd_ref[0], vd_ref[0], col < row, _suffix_ones(tq))

    ones_p = _suffix_ones(tk)

    def cond(state):
        j, cm = state
        return jnp.logical_and(j >= 0, cm <= SB_ZERO_LOG)

    def body(state):
        j, _ = state
        start = pl.multiple_of(j * tk, tk)
        if past_transposed:
            k = kp_ref[:, pl.ds(start, tk)].astype(BF16)
            v = vp_ref[:, pl.ds(start, tk)].astype(BF16)
        else:
            k = kp_ref[pl.ds(start, tk), :].astype(BF16)
            v = vp_ref[pl.ds(start, tk), :].astype(BF16)
        return j - 1, block(k, v, None, ones_p, past_transposed)

    lax.while_loop(cond, body, (past_blocks(qi) - 1, cmin))
    o_ref[0] = acc_ref[...].astype(BF16)


def _sb_attention(qb, kb, vb, cache, tq, tk):
    b, t, _ = qb.shape
    tile = pl.BlockSpec((1, tq, LANES), lambda bi, hi, qi: (bi, qi, hi))
    if cache is None:
        assert t % tq == 0 and tq % tk == 0
        past_blocks = lambda qi: qi * (tq // tk)
        full = pl.BlockSpec((None, t, LANES), lambda bi, hi, qi: (bi, 0, hi))
        k_past, v_past = kb, vb
    else:
        k_past, v_past, li = cache
        p = k_past.shape[3]
        assert t == tq and p % tk == 0
        past_blocks = lambda qi: jnp.int32(p // tk)
        full = pl.BlockSpec((None, None, LANES, p), lambda bi, hi, qi: (li, bi, hi, 0))
    return pl.pallas_call(
        functools.partial(_sb_kernel, tq=tq, tk=tk, past_blocks=past_blocks,
                          past_transposed=cache is not None),
        grid=(b, SB_DIM // LANES, t // tq),
        in_specs=[tile, tile, tile, full, full],
        out_specs=tile,
        out_shape=jax.ShapeDtypeStruct((b, t, SB_DIM), BF16),
        scratch_shapes=[
            pltpu.VMEM((tq, LANES), F32),
            pltpu.VMEM((tq, 1), F32),
            pltpu.VMEM((tq, 1), F32),
        ],
        compiler_params=_cparams(3),
        name="sb_attn",
    )(qb, kb, vb, k_past, v_past)


def _exp2_rows(s, m):
    n = s.shape[1] // LANES
    if n == 0:
        return jnp.exp2(s - m[:, :s.shape[1]]).astype(BF16)
    return jnp.concatenate(
        [jnp.exp2(s[:, c * LANES:(c + 1) * LANES] - m).astype(BF16) for c in range(n)], axis=1)


def _diff_kernel(*refs, tq, tk, row_chunk, head_stride, diag_in_past, past_blocks, q_pos0,
                 lam_init):
    if diag_in_past:
        q_ref, kp_ref, vp_ref = refs[:3]
        kd_ref = vd_ref = None
        rest = refs[3:]
    else:
        q_ref, kd_ref, vd_ref, kp_ref, vp_ref = refs[:5]
        rest = refs[5:]
    (lq1_ref, lk1_ref, lq2_ref, lk2_ref, g_ref, o_ref, m_ref, a_ref, p_ref, alpha_ref) = rest
    qi = pl.program_id(2)
    lane = lax.broadcasted_iota(jnp.int32, (tq, LANES), 1)
    first = lane < DIFF_HALF_DIM
    qf = q_ref[0].astype(F32)
    qs = (jnp.where(first, qf, 0.0).astype(BF16), jnp.where(first, 0.0, qf).astype(BF16))
    n_max = kp_ref.shape[0] // (tk * head_stride)
    rows = min(tq, row_chunk)

    def with_ones(v):
        return jnp.concatenate([v, jnp.ones(v.shape, BF16)], axis=1)

    def past(ref, j):
        start = jnp.clip(j, 0, n_max - 1) * tk
        if head_stride == 1:
            return ref[pl.ds(pl.multiple_of(start, tk), tk), :].astype(BF16)
        first_row = start * head_stride + pl.program_id(1)
        return ref[pl.ds(first_row, tk, stride=head_stride), :].astype(BF16)

    def own_mask(r, n_rows, n_keys):
        qpos = q_pos0(qi) + r + lax.broadcasted_iota(jnp.int32, (n_rows, n_keys), 0)
        kpos = q_pos0(qi) + lax.broadcasted_iota(jnp.int32, (n_rows, n_keys), 1)
        return (kpos // CHUNK) <= (qpos // CHUNK)

    def step(k, v1, own, pending_own=False):
        for r in range(0, tq, rows):
            sl = slice(r, r + rows)
            for h in range(2):
                if v1 is not None:
                    nk = r + rows if pending_own else tk
                    alpha = alpha_ref[h, sl]
                    a_ref[h, sl] = (jnp.concatenate([alpha, alpha], axis=1) * a_ref[h, sl]
                                    + _dot(p_ref[h, sl, :nk], v1[:nk]))
                if k is not None:
                    nk = r + rows if own else tk
                    s = _dot_nt(qs[h][sl], k[:nk])
                    if own:
                        s = jnp.where(own_mask(r, rows, nk), s, NEG_INF)
                    m_old = m_ref[h, sl]
                    m_new = jnp.maximum(m_old, jnp.max(s, axis=-1, keepdims=True))
                    alpha_ref[h, sl] = jnp.exp2(m_old - m_new)
                    m_ref[h, sl] = m_new
                    p_ref[h, sl, :nk] = _exp2_rows(s, m_new)

    n_past = past_blocks(qi)
    a_ref[...] = jnp.zeros_like(a_ref)
    if diag_in_past:
        m_ref[...] = jnp.full(m_ref.shape, NEG_INF, F32)
    else:
        for h in range(2):
            s = jnp.where(own_mask(0, tq, tq), _dot_nt(qs[h], kd_ref[0]), NEG_INF)
            m = jnp.broadcast_to(jnp.max(s, axis=-1, keepdims=True), (tq, LANES))
            m_ref[h] = m
            a_ref[h] = _dot(_exp2_rows(s, m), with_ones(vd_ref[0]))
    alpha_ref[...] = jnp.ones_like(alpha_ref)

    @pl.when((pl.program_id(0) == 0) & (pl.program_id(1) == 0) & (qi == 0))
    def _():
        p_ref[...] = jnp.zeros_like(p_ref)

    def pending_values(j):
        v1 = with_ones(past(vp_ref, j))
        return jnp.where(j >= 0, v1, jnp.zeros_like(v1))

    def body(j, carry):
        step(past(kp_ref, j), pending_values(j - 1), False)
        return carry

    lax.fori_loop(0, n_past, body, 0)
    if diag_in_past:
        step(past(kp_ref, n_past), pending_values(n_past - 1), True)
        step(None, with_ones(past(vp_ref, n_past)), False, pending_own=True)
    else:
        step(None, with_ones(past(vp_ref, n_past - 1)), False)

    a1_ref, a2_ref = a_ref.at[0], a_ref.at[1]
    lam = (jnp.exp(jnp.sum(lq1_ref[...] * lk1_ref[...], axis=-1, keepdims=True))
           - jnp.exp(jnp.sum(lq2_ref[...] * lk2_ref[...], axis=-1, keepdims=True)) + lam_init)
    o = (a1_ref[:, :LANES] / a1_ref[:, LANES:]
         - lam * (a2_ref[:, :LANES] / a2_ref[:, LANES:]))
    o_ref[0] = (_rms(o, g_ref[...]) * (1.0 - lam_init)).astype(BF16)


def _diff_attention(qb, kb, vb, cache, lam_params, diff_g, tq, tk, lam_init):
    b, t, _ = qb.shape
    tile = pl.BlockSpec((1, tq, LANES), lambda bi, hi, qi: (bi, qi, hi))
    vec64 = pl.BlockSpec((1, DIFF_HALF_DIM), lambda bi, hi, qi: (0, 0))
    vec128 = pl.BlockSpec((1, DIFF_HEAD_DIM), lambda bi, hi, qi: (0, 0))
    if cache is None:
        assert tq == tk and t % tq == 0
        past_len, head_stride = 0, 1
        past_blocks = lambda qi: qi
        full = pl.BlockSpec((None, t, LANES), lambda bi, hi, qi: (bi, 0, hi))
        kv_specs, kv_args = [full, full], (kb, vb)
    else:
        k_cache, v_cache, li = cache
        rows = k_cache.shape[2]
        past_len, head_stride = rows // DIFF_HEADS, DIFF_HEADS
        assert t == tq and past_len % tk == 0
        past_blocks = lambda qi: past_len // tk
        full = pl.BlockSpec((None, None, rows, LANES), lambda bi, hi, qi: (li, bi, 0, 0))
        kv_specs, kv_args = [tile, tile, full, full], (kb, vb, k_cache, v_cache)
    q_pos0 = lambda qi: past_len + qi * tq
    return pl.pallas_call(
        functools.partial(_diff_kernel, tq=tq, tk=tk, row_chunk=128, head_stride=head_stride,
                          diag_in_past=cache is None, past_blocks=past_blocks, q_pos0=q_pos0,
                          lam_init=lam_init),
        grid=(b, DIFF_HEADS, t // tq),
        in_specs=[tile] + kv_specs + [vec64, vec64, vec64, vec64, vec128],
        out_specs=tile,
        out_shape=jax.ShapeDtypeStruct((b, t, DIFF_DIM), BF16),
        scratch_shapes=[
            pltpu.VMEM((2, tq, LANES), F32), pltpu.VMEM((2, tq, 2 * LANES), F32),
            pltpu.VMEM((2, tq, tk), BF16), pltpu.VMEM((2, tq, LANES), F32),
        ],
        compiler_params=pltpu.CompilerParams(
            dimension_semantics=("arbitrary",) * 3, vmem_limit_bytes=VMEM_LIMIT_BYTES),
        name="diff_attn",
    )(qb, *kv_args,
      *[a.reshape(1, DIFF_HALF_DIM) for a in lam_params], diff_g.reshape(1, DIFF_HEAD_DIM))


def _memkv_kernel(m_ref, g_ref, wk_ref, wv_ref, k_ref, v_ref):
    hb = _rms(m_ref[...], g_ref[...]).astype(BF16)
    k_ref[...] = _dot(hb, wk_ref[...])
    v_ref[...] = _dot(hb, wv_ref[...])


def _memkv(mem2d, g, wk_b, wv_b, li, tm):
    n = mem2d.shape[0]
    assert n % tm == 0
    row = pl.BlockSpec((tm, D_MODEL), lambda i: (i, 0))
    return pl.pallas_call(
        _memkv_kernel,
        grid=(n // tm,),
        in_specs=[row, pl.BlockSpec((1, D_MODEL), lambda i: (0, 0)),
                  _layer_weight(wk_b, li), _layer_weight(wv_b, li)],
        out_specs=[row, row],
        out_shape=[jax.ShapeDtypeStruct((n, D_MODEL), F32)] * 2,
        compiler_params=_cparams(1),
        name="memkv",
    )(mem2d, g.reshape(1, D_MODEL), wk_b, wv_b)


def _mix_mem_kernel(x_ref, pool_ref, sb_ref, df_ref, wout_ref, gpost0_ref, gpre1_ref, gpost1_ref,
                    wq_ref, mk_ref, mv_ref, wo_ref, o_ref, att_ref, mkb_ref, mvb_ref, *, sub):
    @pl.when(pl.program_id(1) == 0)
    def _():
        for h in range(MEM_HEADS):
            sl = slice(h * MEM_HEAD_DIM, (h + 1) * MEM_HEAD_DIM)
            mkb_ref[:, sl] = mk_ref[:, h, :].astype(BF16)
            mvb_ref[:, sl] = mv_ref[:, h, :].astype(BF16)

    for r in range(0, x_ref.shape[1], sub):
        rows = slice(r, r + sub)
        mixed = (_dot(pool_ref[0, rows, :], wout_ref[0:POOL_DIM, :])
                 + _dot(sb_ref[0, rows, :], wout_ref[POOL_DIM:POOL_DIM + SB_DIM, :])
                 + _dot(df_ref[0, rows, :], wout_ref[POOL_DIM + SB_DIM:, :]))
        x1 = x_ref[0, rows, :] + _rms(mixed, gpost0_ref[...])

        hb = _rms(x1, gpre1_ref[...]).astype(BF16)
        qm = (_dot(hb, wq_ref[...]) * (MEM_HEAD_DIM ** -0.5)).astype(BF16)
        for h in range(MEM_HEADS):
            sl = slice(h * MEM_HEAD_DIM, (h + 1) * MEM_HEAD_DIM)
            s = _dot_nt(qm[:, sl], mkb_ref[:, sl])
            e = jnp.exp(s - jnp.max(s, axis=-1, keepdims=True))
            p = e / jnp.sum(e, axis=-1, keepdims=True)
            att_ref[rows, sl] = _dot(p.astype(BF16), mvb_ref[:, sl]).astype(BF16)
        y = _dot(att_ref[rows, :], wo_ref[...])
        o_ref[0, rows, :] = x1 + _rms(y, gpost1_ref[...])


def _mix_mem(x, pool_o, sb_o, df_o, wout_b, g_post0, g_pre1, g_post1, wq_b, mem, wo_b, li, tm):
    b, t, _ = x.shape
    assert t % tm == 0
    mk, mv = mem[0], mem[1]
    m = mk.shape[-3]
    tok = lambda w: pl.BlockSpec((1, tm, w), lambda bi, ti: (bi, ti, 0))
    vec = pl.BlockSpec((1, D_MODEL), lambda bi, ti: (0, 0))
    if len(mem) == 2:
        memspec = pl.BlockSpec((None, m, MEM_HEADS, MEM_HEAD_DIM), lambda bi, ti: (bi, 0, 0, 0))
    else:
        mem_li = mem[2]
        memspec = pl.BlockSpec((None, None, m, MEM_HEADS, MEM_HEAD_DIM),
                               lambda bi, ti: (mem_li, bi, 0, 0, 0))
    return pl.pallas_call(
        functools.partial(_mix_mem_kernel, sub=min(tm, 256)),
        grid=(b, t // tm),
        in_specs=[tok(D_MODEL), tok(POOL_DIM), tok(SB_DIM), tok(DIFF_DIM),
                  _layer_weight(wout_b, li), vec, vec, vec,
                  _layer_weight(wq_b, li), memspec, memspec, _layer_weight(wo_b, li)],
        out_specs=tok(D_MODEL),
        out_shape=jax.ShapeDtypeStruct((b, t, D_MODEL), F32),
        scratch_shapes=[pltpu.VMEM((tm, D_MODEL), BF16),
                        pltpu.VMEM((m, D_MODEL), BF16), pltpu.VMEM((m, D_MODEL), BF16)],
        compiler_params=pltpu.CompilerParams(
            dimension_semantics=("parallel", "arbitrary"), vmem_limit_bytes=VMEM_LIMIT_BYTES),
        name="mix_mem",
    )(x, pool_o, sb_o, df_o, wout_b, g_post0.reshape(1, -1), g_pre1.reshape(1, -1),
      g_post1.reshape(1, -1), wq_b, mk, mv, wo_b)


def _ffn_kernel(x_ref, gpre_ref, gpost_ref, wg_ref, wu_ref, wd_ref, o_ref, *, fc, sub):
    for r in range(0, x_ref.shape[0], sub):
        x = x_ref[r:r + sub, :]
        hb = _rms(x, gpre_ref[...]).astype(BF16)
        acc = jnp.zeros(x.shape, F32)
        for c in range(D_FF // fc):
            sl = slice(c * fc, (c + 1) * fc)
            gate = _dot(hb, wg_ref[:, sl])
            up = _dot(hb, wu_ref[:, sl])
            act = (gate / (1.0 + jnp.exp(-gate))) * up
            acc = acc + _dot(act.astype(BF16), wd_ref[sl, :])
        o_ref[r:r + sub, :] = x + _rms(acc, gpost_ref[...])


def _ffn(x2d, g_pre2, g_post2, wg_b, wu_b, wd_b, li, tm, fc=256):
    n = x2d.shape[0]
    assert n % tm == 0 and D_FF % fc == 0
    row = pl.BlockSpec((tm, D_MODEL), lambda i: (i, 0))
    vec = pl.BlockSpec((1, D_MODEL), lambda i: (0, 0))
    return pl.pallas_call(
        functools.partial(_ffn_kernel, fc=fc, sub=min(tm, 256)),
        grid=(n // tm,),
        in_specs=[row, vec, vec, _layer_weight(wg_b, li), _layer_weight(wu_b, li),
                  _layer_weight(wd_b, li)],
        out_specs=row,
        out_shape=jax.ShapeDtypeStruct((n, D_MODEL), F32),
        compiler_params=_cparams(1),
        name="ffn",
    )(x2d, g_pre2.reshape(1, -1), g_post2.reshape(1, -1), wg_b, wu_b, wd_b)


def _pick(n, pref):
    return pref if n % pref == 0 else n


def _layer(x, pool_buf, sb_cache, diff_cache, mem, lw, li, lam_init):
    (g_pre, g_post, w_in_b, w_out_b, wbd_b, pool_scale, lam_params, diff_g,
     wq_b, wo_b, wg_b, wu_b, wd_b) = lw
    b, t, _ = x.shape
    n = b * t
    past = 0 if sb_cache is None else sb_cache[0].shape[3]

    (u, sk, sv, dk, dv, sqb, skb, svb, dqb, dkb, dvb) = _inproj(
        x, g_pre[0], w_in_b, li, _pick(n, 512))
    r3 = lambda a: a.reshape(b, t, a.shape[-1])
    u, sqb, skb, svb, dqb, dkb, dvb = map(r3, (u, sqb, skb, svb, dqb, dkb, dvb))

    buf16 = jnp.concatenate([jnp.zeros((b, 1, POOL_DIM), F32), pool_buf], axis=1)
    pool_o = _pool(u, buf16, wbd_b, pool_scale, _pick(t, 512), past)
    new_pool = jnp.concatenate([pool_buf, u], axis=1)[:, -POOL_STATE:]

    if past:
        sb_o = _sb_attention(sqb, skb, svb, sb_cache, t, _pick(past, 256))
        df_o = _diff_attention(dqb, dkb, dvb, diff_cache, lam_params, diff_g,
                               t, _pick(past, 512), lam_init)
    else:
        sb_tq = _pick(t, 256)
        sb_o = _sb_attention(sqb, skb, svb, None, sb_tq, sb_tq)
        df_tq = _pick(t, 1024)
        df_o = _diff_attention(dqb, dkb, dvb, None, lam_params, diff_g, df_tq, df_tq, lam_init)

    x = _mix_mem(x, pool_o, sb_o, df_o, w_out_b, g_post[0], g_pre[1], g_post[1],
                 wq_b, mem, wo_b, li, _pick(t, 512))
    x = _ffn(x.reshape(n, D_MODEL), g_pre[2], g_post[2], wg_b, wu_b, wd_b, li,
             _pick(n, 512)).reshape(b, t, D_MODEL)

    if sk.ndim == 3:
        to_leaf = lambda a: a.reshape(b, SB_HEADS, SB_HEAD_DIM, t).transpose(0, 3, 1, 2)
    else:
        to_leaf = lambda a: a.reshape(b, t, SB_HEADS, SB_HEAD_DIM)
    dk = dk.reshape(b, t, DIFF_HEADS, DIFF_HEAD_DIM)
    dv = dv.reshape(b, t, DIFF_HEADS, DIFF_HEAD_DIM)
    return x, to_leaf(sk), to_leaf(sv), dk, dv, new_pool


def _block_diag(pool_w):
    g, c, d = pool_w.shape
    out = jnp.zeros((g * c, g * d), pool_w.dtype)
    for i in range(g):
        out = lax.dynamic_update_slice(out, pool_w[i], (i * c, i * d))
    return out


def kernel(x_prompt, x_sample, cache_sb_k, cache_sb_v, cache_diff_k, cache_diff_v, cache_mem_k, cache_mem_v, state_pool, mem_prompt, g_pre, g_post, g_mem, w_in, w_out, pool_w, pool_scale, lam_q1, lam_k1, lam_q2, lam_k2, diff_g, wq_m, wk_m, wv_m, wo_m, w_gate, w_up, w_down):
    depth = w_in.shape[0]
    xp, xs = x_prompt, x_sample
    bp, mem_len = mem_prompt.shape[0], mem_prompt.shape[1]
    dshape = cache_diff_k.shape
    diff_k_rows = cache_diff_k.reshape(dshape[0], dshape[1], dshape[2] * dshape[3], dshape[4])
    diff_v_rows = cache_diff_v.reshape(dshape[0], dshape[1], dshape[2] * dshape[3], dshape[4])
    sshape = cache_sb_k.shape
    to_feature_major = lambda c: c.transpose(0, 1, 3, 4, 2).reshape(
        sshape[0], sshape[1], sshape[3] * sshape[4], sshape[2])
    sb_kT, sb_vT = to_feature_major(cache_sb_k), to_feature_major(cache_sb_v)
    (w_in_b, w_out_b, wq_b, wk_b, wv_b, wo_b, wg_b, wu_b, wd_b) = [
        w.astype(BF16) for w in (w_in, w_out, wq_m, wk_m, wv_m, wo_m, w_gate, w_up, w_down)]
    p_sbk, p_sbv, p_dk, p_dv, p_pool, p_mk, p_mv = [], [], [], [], [], [], []
    s_sbk, s_sbv, s_dk, s_dv, s_pool = [], [], [], [], []
    for li in range(depth):
        lam_init = 0.8 - 0.6 * math.exp(-0.3 * li)
        lw = (g_pre[li], g_post[li], w_in_b, w_out_b, _block_diag(pool_w[li]).astype(BF16),
              pool_scale[li], (lam_q1[li], lam_k1[li], lam_q2[li], lam_k2[li]), diff_g[li],
              wq_b, wo_b, wg_b, wu_b, wd_b)
        mk2d, mv2d = _memkv(mem_prompt.reshape(bp * mem_len, D_MODEL), g_mem[li],
                            wk_b, wv_b, li, _pick(bp * mem_len, 256))
        mk = mk2d.reshape(bp, mem_len, MEM_HEADS, MEM_HEAD_DIM)
        mv = mv2d.reshape(bp, mem_len, MEM_HEADS, MEM_HEAD_DIM)
        zero_buf = jnp.zeros((xp.shape[0], POOL_STATE, POOL_DIM), xp.dtype)
        xp, sk, sv, dk, dv, npool = _layer(xp, zero_buf, None, None, (mk, mv), lw, li, lam_init)
        p_sbk.append(sk); p_sbv.append(sv); p_dk.append(dk); p_dv.append(dv)
        p_pool.append(npool); p_mk.append(mk); p_mv.append(mv)
        xs, sk, sv, dk, dv, npool = _layer(xs, state_pool[li], (sb_kT, sb_vT, li),
                                           (diff_k_rows, diff_v_rows, li),
                                           (cache_mem_k, cache_mem_v, li), lw, li, lam_init)
        s_sbk.append(sk); s_sbv.append(sv); s_dk.append(dk); s_dv.append(dv); s_pool.append(npool)
    return (xp, xs,
            jnp.stack(p_sbk), jnp.stack(p_sbv), jnp.stack(p_dk), jnp.stack(p_dv),
            jnp.stack(p_pool), jnp.stack(p_mk), jnp.stack(p_mv),
            jnp.stack(s_sbk), jnp.stack(s_sbv), jnp.stack(s_dk), jnp.stack(s_dv),
            jnp.stack(s_pool))
```

```python
import functools
import math

import jax
import jax.numpy as jnp
from jax import lax
from jax.experimental import pallas as pl
from jax.experimental.pallas import tpu as pltpu

F32 = jnp.float32
BF16 = jnp.bfloat16

D_MODEL = 1024
CHUNK = 64
POOL_WINDOWS = (2, 4, 8, 16)
POOL_GROUP_DIM = 64
POOL_DIM = 256
POOL_STATE = 15
SB_HEADS = 4
SB_HEAD_DIM = 64
SB_DIM = 256
DIFF_HEADS = 4
DIFF_HALF_DIM = 64
DIFF_HEAD_DIM = 128
DIFF_DIM = 512
MEM_HEADS = 4
MEM_HEAD_DIM = 256
D_FF = 2816
EPS = 1e-6
NEG_INF = -1e30
LOG2E = 1.4426950408889634

LANES = 128
MXU_DIM = 256
VMEM_LIMIT_BYTES = 56 * 1024 * 1024

SB_ZERO_LOG = 110.0


def _cparams(n_grid):
    return pltpu.CompilerParams(
        dimension_semantics=("parallel",) * n_grid,
        vmem_limit_bytes=VMEM_LIMIT_BYTES,
    )


def _rms(x, g):
    return x * lax.rsqrt(jnp.mean(x * x, axis=-1, keepdims=True) + EPS) * g


def _dot(a, b):
    return jnp.dot(a, b, preferred_element_type=F32)


def _dot_nt(a, b):
    return lax.dot_general(a, b, (((1,), (1,)), ((), ())), preferred_element_type=F32)


def _pool_tile(u, xs_ref, pos0, wbd, scale):
    tm = u.shape[0]
    xs_ref[16:, :] = u
    pos = pos0 + lax.broadcasted_iota(jnp.int32, (tm, POOL_DIM), 0)
    group = lax.broadcasted_iota(jnp.int32, (tm, POOL_DIM), 1) // POOL_GROUP_DIM
    mean = jnp.zeros((tm, POOL_DIM), F32)
    run = u
    gi = 0
    for back in range(1, max(POOL_WINDOWS)):
        run = run + xs_ref[16 - back:16 - back + tm, :]
        if back + 1 == POOL_WINDOWS[gi]:
            cnt = jnp.minimum(back + 1, pos + 1).astype(F32)
            mean = jnp.where(group == gi, run / cnt, mean)
            gi += 1
    d = (mean - u).astype(BF16)
    return (_dot(d, wbd) * scale).astype(BF16)


def _inproj_kernel(*refs, transpose_kv, n_prev):
    x_ref, g_ref, w_ref = refs[:3]
    prev = refs[3:7] if n_prev else None
    (u_ref, sk_ref, sv_ref, dk_ref, dv_ref,
     sqb_ref, skb_ref, svb_ref, dqb_ref, dkb_ref, dvb_ref) = refs[-11:]
    if n_prev:
        for dst, src in zip((sk_ref, sv_ref, dk_ref, dv_ref), prev):
            dst[0:n_prev] = src[...]
    hb = _rms(x_ref[...], g_ref[...]).astype(BF16)

    def cols(lo, hi):
        return _dot(hb, w_ref[:, lo:hi])

    u_ref[...] = cols(0, 256)
    sqb_ref[...] = (cols(256, 512) * (LOG2E * SB_HEAD_DIM ** -0.5)).astype(BF16)
    sk = cols(512, 768)
    skb_ref[...] = sk.astype(BF16)
    sv = cols(768, 1024)
    svb_ref[...] = sv.astype(BF16)
    if transpose_kv:
        sk_ref[n_prev, 0] = sk.T
        sv_ref[n_prev, 0] = sv.T
    else:
        sk_ref[n_prev] = sk
        sv_ref[n_prev] = sv
    dqb_ref[...] = (cols(1024, 1536) * (LOG2E * DIFF_HALF_DIM ** -0.5)).astype(BF16)
    tm = x_ref.shape[0]
    dk = cols(1536, 2048)
    dkb_ref[...] = dk.astype(BF16)
    dv = cols(2048, 2560)
    dvb_ref[...] = dv.astype(BF16)
    for h in range(DIFF_HEADS):
        sl = slice(h * DIFF_HEAD_DIM, (h + 1) * DIFF_HEAD_DIM)
        dk_ref[n_prev, pl.ds(h, tm, stride=DIFF_HEADS), :] = dk[:, sl]
        dv_ref[n_prev, pl.ds(h, tm, stride=DIFF_HEADS), :] = dv[:, sl]


def _layer_weight(w, li):
    return pl.BlockSpec((None,) + w.shape[1:], lambda *_: (li, 0, 0),
                        pipeline_mode=pl.Buffered(1))


def _inproj(x, g, w_in_b, li, tm, prev):
    b, t, _ = x.shape
    n = b * t
    assert n % tm == 0
    n_prev = 0 if prev is None else prev[0].shape[0]
    nl = n_prev + 1
    transpose_kv = t % tm == 0 and tm % LANES == 0
    widths = (256, 256, 256, 512, 512, 256, 256, 256, 512, 512, 512)
    dtypes = (F32,) * 5 + (BF16,) * 6
    out_shape = [jax.ShapeDtypeStruct((n, w), dt) for w, dt in zip(widths, dtypes)]
    out_specs = [pl.BlockSpec((tm, w), lambda i: (i, 0)) for w in widths]

    def leaf(layers, kind):
        if kind == "d":
            return ((layers, n * DIFF_HEADS, DIFF_HEAD_DIM),
                    pl.BlockSpec((layers, tm * DIFF_HEADS, DIFF_HEAD_DIM), lambda i: (0, i, 0)))
        if transpose_kv:
            per_batch = t // tm
            return ((layers, b, SB_DIM, t),
                    pl.BlockSpec((layers, 1, SB_DIM, tm),
                                 lambda i: (0, i // per_batch, 0, i % per_batch)))
        return (layers, n, SB_DIM), pl.BlockSpec((layers, tm, SB_DIM), lambda i: (0, i, 0))

    kinds = ("s", "s", "d", "d")
    for i, kind in zip((1, 2, 3, 4), kinds):
        shape, out_specs[i] = leaf(nl, kind)
        out_shape[i] = jax.ShapeDtypeStruct(shape, F32)
    prev_specs = [] if prev is None else [leaf(n_prev, kind)[1] for kind in kinds]
    return pl.pallas_call(
        functools.partial(_inproj_kernel, transpose_kv=transpose_kv, n_prev=n_prev),
        grid=(n // tm,),
        in_specs=[
            pl.BlockSpec((tm, D_MODEL), lambda i: (i, 0)),
            pl.BlockSpec((1, D_MODEL), lambda i: (0, 0)),
            _layer_weight(w_in_b, li),
        ] + prev_specs,
        out_specs=out_specs,
        out_shape=out_shape,
        compiler_params=_cparams(1),
        name="inproj",
    )(x.reshape(n, D_MODEL), g.reshape(1, D_MODEL), w_in_b, *(prev or ()))


def _pool_kernel(u_ref, prev_ref, buf_ref, wbd_ref, scale_ref, o_ref, xs_ref, *, tm, start_pos):
    t = pl.program_id(1)
    xs_ref[0:16, :] = jnp.where(t == 0, buf_ref[0], prev_ref[0])
    o_ref[0] = _pool_tile(u_ref[0], xs_ref, start_pos + t * tm, wbd_ref[...], scale_ref[...])


def _pool(u, buf16, wbd_b, pool_scale, tm, start_pos):
    b, t, _ = u.shape
    assert t % tm == 0 and tm % 16 == 0
    r = tm // 16
    return pl.pallas_call(
        functools.partial(_pool_kernel, tm=tm, start_pos=start_pos),
        grid=(b, t // tm),
        in_specs=[
            pl.BlockSpec((1, tm, POOL_DIM), lambda bi, ti: (bi, ti, 0)),
            pl.BlockSpec((1, 16, POOL_DIM), lambda bi, ti: (bi, jnp.maximum(ti * r - 1, 0), 0)),
            pl.BlockSpec((1, 16, POOL_DIM), lambda bi, ti: (bi, 0, 0)),
            pl.BlockSpec((POOL_DIM, POOL_DIM), lambda bi, ti: (0, 0)),
            pl.BlockSpec((1, POOL_DIM), lambda bi, ti: (0, 0)),
        ],
        out_specs=pl.BlockSpec((1, tm, POOL_DIM), lambda bi, ti: (bi, ti, 0)),
        out_shape=jax.ShapeDtypeStruct((b, t, POOL_DIM), BF16),
        scratch_shapes=[pltpu.VMEM((tm + 16, POOL_DIM), F32)],
        compiler_params=_cparams(2),
        name="pool",
    )(u, u, buf16, wbd_b, pool_scale.reshape(1, POOL_DIM))


def _suffix_ones(n):
    r = lax.broadcasted_iota(jnp.int32, (n, n), 0)
    c = lax.broadcasted_iota(jnp.int32, (n, n), 1)
    return jnp.where(r >= c, 1.0, 0.0).astype(BF16)


def _sb_kernel(q_ref, kd_ref, vd_ref, kp_ref, vp_ref, o_ref, acc_ref, ca_ref, cb_ref,
               *, tq, tk, past_blocks, past_transposed):
    qi = pl.program_id(2)
    lane = lax.broadcasted_iota(jnp.int32, (tq, LANES), 1)
    first = lane < SB_HEAD_DIM
    qf = q_ref[0].astype(F32)
    qa = jnp.where(first, qf, 0.0).astype(BF16)
    qb = jnp.where(first, 0.0, qf).astype(BF16)

    def head(qh, k, mask, ones, transposed):
        z = _dot(qh, k) if transposed else _dot_nt(qh, k)
        sp = jnp.maximum(z, 0.0) + jnp.log2(1.0 + jnp.exp2(-jnp.abs(z)))
        if mask is not None:
            sp = jnp.where(mask, sp, 0.0)
        hi = sp.astype(BF16)
        lo = (sp - hi.astype(F32)).astype(BF16)
        return z, _dot(hi, ones) + _dot(lo, ones)

    def weigh(za, csa, zb, csb, ca, cb, v, mask, transposed):
        wa = jnp.exp2(za - csa - ca)
        wb = jnp.exp2(zb - csb - cb)
        if mask is not None:
            wa = jnp.where(mask, wa, 0.0)
            wb = jnp.where(mask, wb, 0.0)
        apply = _dot_nt if transposed else _dot
        pv = jnp.where(first, apply(wa.astype(BF16), v), apply(wb.astype(BF16), v))
        return pv, ca + csa[:, 0:1], cb + csb[:, 0:1]

    def past(j):
        start = pl.multiple_of(jnp.maximum(j, 0) * tk, tk)
        if past_transposed:
            return (kp_ref[:, pl.ds(start, tk)].astype(BF16),
                    vp_ref[:, pl.ds(start, tk)].astype(BF16))
        return kp_ref[pl.ds(start, tk), :].astype(BF16), vp_ref[pl.ds(start, tk), :].astype(BF16)

    n_past = past_blocks(qi)
    ones_p = _suffix_ones(tk)
    row = lax.broadcasted_iota(jnp.int32, (tq, tq), 0)
    col = lax.broadcasted_iota(jnp.int32, (tq, tq), 1)
    own = col < row
    have_prev = n_past > 0
    kd, vd = kd_ref[0], vd_ref[0]
    k1, v1 = past(n_past - 1)
    za, csa = head(qa, kd, own, _suffix_ones(tq), False)
    zb, csb = head(qb, kd, own, _suffix_ones(tq), False)
    za1, csa1 = head(qa, k1, have_prev, ones_p, past_transposed)
    zb1, csb1 = head(qb, k1, have_prev, ones_p, past_transposed)
    pv0, ca, cb = weigh(za, csa, zb, csb, 0.0, 0.0, vd, own, False)
    pv1, ca, cb = weigh(za1, csa1, zb1, csb1, ca, cb, v1, have_prev, past_transposed)
    acc_ref[...] = pv0 + pv1
    ca_ref[...] = ca
    cb_ref[...] = cb

    def cond(state):
        j, cm = state
        return jnp.logical_and(j >= 0, cm <= SB_ZERO_LOG * LOG2E)

    def body(state):
        j, _ = state
        k, v = past(j)
        za, csa = head(qa, k, None, ones_p, past_transposed)
        zb, csb = head(qb, k, None, ones_p, past_transposed)
        pv, ca, cb = weigh(za, csa, zb, csb, ca_ref[...], cb_ref[...], v, None, past_transposed)
        acc_ref[...] += pv
        ca_ref[...] = ca
        cb_ref[...] = cb
        return j - 1, jnp.min(jnp.minimum(ca, cb))

    lax.while_loop(cond, body, (n_past - 2, jnp.min(jnp.minimum(ca, cb))))
    o_ref[0] = acc_ref[...].astype(BF16)


def _sb_attention(qb, kb, vb, cache, tq, tk):
    b, t, _ = qb.shape
    tile = pl.BlockSpec((1, tq, LANES), lambda bi, hi, qi: (bi, qi, hi))
    if cache is None:
        assert t % tq == 0 and tq % tk == 0
        past_blocks = lambda qi: qi * (tq // tk)
        full = pl.BlockSpec((None, t, LANES), lambda bi, hi, qi: (bi, 0, hi))
        k_past, v_past = kb, vb
    else:
        k_past, v_past, li = cache
        p = k_past.shape[3]
        assert t == tq and p % tk == 0
        past_blocks = lambda qi: jnp.int32(p // tk)
        full = pl.BlockSpec((None, None, LANES, p), lambda bi, hi, qi: (li, bi, hi, 0))
    return pl.pallas_call(
        functools.partial(_sb_kernel, tq=tq, tk=tk, past_blocks=past_blocks,
                          past_transposed=cache is not None),
        grid=(b, SB_DIM // LANES, t // tq),
        in_specs=[tile, tile, tile, full, full],
        out_specs=tile,
        out_shape=jax.ShapeDtypeStruct((b, t, SB_DIM), BF16),
        scratch_shapes=[
            pltpu.VMEM((tq, LANES), F32),
            pltpu.VMEM((tq, 1), F32),
            pltpu.VMEM((tq, 1), F32),
        ],
        compiler_params=_cparams(3),
        name="sb_attn",
    )(qb, kb, vb, k_past, v_past)


def _exp2_rows(s, m):
    n = s.shape[1] // LANES
    if n == 0:
        return jnp.exp2(s - m[:, :s.shape[1]]).astype(BF16)
    return jnp.concatenate(
        [jnp.exp2(s[:, c * LANES:(c + 1) * LANES] - m).astype(BF16) for c in range(n)], axis=1)


def _diff_kernel(*refs, tq, tk, row_chunk, head_stride, diag_in_past, past_blocks, q_pos0,
                 lam_init):
    if diag_in_past:
        q_ref, kp_ref, vp_ref = refs[:3]
        kd_ref = vd_ref = None
        rest = refs[3:]
    else:
        q_ref, kd_ref, vd_ref, kp_ref, vp_ref = refs[:5]
        rest = refs[5:]
    (lq1_ref, lk1_ref, lq2_ref, lk2_ref, g_ref, o_ref, m_ref, a_ref, p_ref, alpha_ref) = rest
    qi = pl.program_id(2)
    lane = lax.broadcasted_iota(jnp.int32, (tq, LANES), 1)
    first = lane < DIFF_HALF_DIM
    qf = q_ref[0].astype(F32)
    qs = (jnp.where(first, qf, 0.0).astype(BF16), jnp.where(first, 0.0, qf).astype(BF16))
    n_max = kp_ref.shape[0] // (tk * head_stride)
    rows = min(tq, row_chunk)

    def with_ones(v):
        return jnp.concatenate([v, jnp.ones(v.shape, BF16)], axis=1)

    def past(ref, j):
        start = jnp.clip(j, 0, n_max - 1) * tk
        if head_stride == 1:
            return ref[pl.ds(pl.multiple_of(start, tk), tk), :].astype(BF16)
        first_row = start * head_stride + pl.program_id(1)
        return ref[pl.ds(first_row, tk, stride=head_stride), :].astype(BF16)

    def own_mask(r, n_rows, n_keys):
        qpos = q_pos0(qi) + r + lax.broadcasted_iota(jnp.int32, (n_rows, n_keys), 0)
        kpos = q_pos0(qi) + lax.broadcasted_iota(jnp.int32, (n_rows, n_keys), 1)
        return (kpos // CHUNK) <= (qpos // CHUNK)

    def step(k, v1, own, pending_own=False):
        for r in range(0, tq, rows):
            sl = slice(r, r + rows)
            for h in range(2):
                if v1 is not None:
                    nk = r + rows if pending_own else tk
                    alpha = alpha_ref[h, sl]
                    a_ref[h, sl] = (jnp.concatenate([alpha, alpha], axis=1) * a_ref[h, sl]
                                    + _dot(p_ref[h, sl, :nk], v1[:nk]))
                if k is not None:
                    nk = r + rows if own else tk
                    s = _dot_nt(qs[h][sl], k[:nk])
                    if own:
                        s = jnp.where(own_mask(r, rows, nk), s, NEG_INF)
                    m_old = m_ref[h, sl]
                    m_new = jnp.maximum(m_old, jnp.max(s, axis=-1, keepdims=True))
                    alpha_ref[h, sl] = jnp.exp2(m_old - m_new)
                    m_ref[h, sl] = m_new
                    p_ref[h, sl, :nk] = _exp2_rows(s, m_new)

    n_past = past_blocks(qi)
    first_step = (pl.program_id(0) == 0) & (pl.program_id(1) == 0) & (qi == 0)
    if diag_in_past:
        m_ref[...] = jnp.full(m_ref.shape, NEG_INF, F32)

        @pl.when(first_step)
        def _():
            a_ref[...] = jnp.zeros_like(a_ref)
            alpha_ref[...] = jnp.ones_like(alpha_ref)
    else:
        for h in range(2):
            s = jnp.where(own_mask(0, tq, tq), _dot_nt(qs[h], kd_ref[0]), NEG_INF)
            m = jnp.broadcast_to(jnp.max(s, axis=-1, keepdims=True), (tq, LANES))
            m_ref[h] = m
            a_ref[h] = _dot(_exp2_rows(s, m), with_ones(vd_ref[0]))
        alpha_ref[...] = jnp.ones_like(alpha_ref)

    @pl.when(first_step)
    def _():
        p_ref[...] = jnp.zeros_like(p_ref)

    def pending_values(j):
        v1 = with_ones(past(vp_ref, j))
        return jnp.where(j >= 0, v1, jnp.zeros_like(v1))

    def body(j, carry):
        step(past(kp_ref, j), pending_values(j - 1), False)
        return carry

    lax.fori_loop(0, n_past, body, 0)
    if diag_in_past:
        step(past(kp_ref, n_past), pending_values(n_past - 1), True)
        step(None, with_ones(past(vp_ref, n_past)), False, pending_own=True)
    else:
        step(None, with_ones(past(vp_ref, n_past - 1)), False)

    a1_ref, a2_ref = a_ref.at[0], a_ref.at[1]
    lam = (jnp.exp(jnp.sum(lq1_ref[...] * lk1_ref[...], axis=-1, keepdims=True))
           - jnp.exp(jnp.sum(lq2_ref[...] * lk2_ref[...], axis=-1, keepdims=True)) + lam_init)
    o = (a1_ref[:, :LANES] / a1_ref[:, LANES:]
         - lam * (a2_ref[:, :LANES] / a2_ref[:, LANES:]))
    o_ref[0] = (_rms(o, g_ref[...]) * (1.0 - lam_init)).astype(BF16)


def _diff_attention(qb, kb, vb, cache, lam_params, diff_g, tq, tk, lam_init):
    b, t, _ = qb.shape
    tile = pl.BlockSpec((1, tq, LANES), lambda bi, hi, qi: (bi, qi, hi))
    vec64 = pl.BlockSpec((1, DIFF_HALF_DIM), lambda bi, hi, qi: (0, 0))
    vec128 = pl.BlockSpec((1, DIFF_HEAD_DIM), lambda bi, hi, qi: (0, 0))
    if cache is None:
        assert tq == tk and t % tq == 0
        past_len, head_stride = 0, 1
        past_blocks = lambda qi: qi
        full = pl.BlockSpec((None, t, LANES), lambda bi, hi, qi: (bi, 0, hi))
        kv_specs, kv_args = [full, full], (kb, vb)
    else:
        k_cache, v_cache, li = cache
        rows = k_cache.shape[2]
        past_len, head_stride = rows // DIFF_HEADS, DIFF_HEADS
        assert t == tq and past_len % tk == 0
        past_blocks = lambda qi: past_len // tk
        full = pl.BlockSpec((None, None, rows, LANES), lambda bi, hi, qi: (li, bi, 0, 0))
        kv_specs, kv_args = [tile, tile, full, full], (kb, vb, k_cache, v_cache)
    q_pos0 = lambda qi: past_len + qi * tq
    return pl.pallas_call(
        functools.partial(_diff_kernel, tq=tq, tk=tk, row_chunk=128, head_stride=head_stride,
                          diag_in_past=cache is None, past_blocks=past_blocks, q_pos0=q_pos0,
                          lam_init=lam_init),
        grid=(b, DIFF_HEADS, t // tq),
        in_specs=[tile] + kv_specs + [vec64, vec64, vec64, vec64, vec128],
        out_specs=tile,
        out_shape=jax.ShapeDtypeStruct((b, t, DIFF_DIM), BF16),
        scratch_shapes=[
            pltpu.VMEM((2, tq, LANES), F32), pltpu.VMEM((2, tq, 2 * LANES), F32),
            pltpu.VMEM((2, tq, tk), BF16), pltpu.VMEM((2, tq, LANES), F32),
        ],
        compiler_params=pltpu.CompilerParams(
            dimension_semantics=("arbitrary",) * 3, vmem_limit_bytes=VMEM_LIMIT_BYTES),
        name="diff_attn",
    )(qb, *kv_args,
      *[a.reshape(1, DIFF_HALF_DIM) for a in lam_params], diff_g.reshape(1, DIFF_HEAD_DIM))


def _memkv_kernel(m_ref, g_ref, wk_ref, wv_ref, k_ref, v_ref):
    hb = _rms(m_ref[...], g_ref[...]).astype(BF16)
    k_ref[...] = _dot(hb, wk_ref[...])
    v_ref[...] = _dot(hb, wv_ref[...])


def _memkv(mem2d, g, wk_b, wv_b, li, tm):
    n = mem2d.shape[0]
    assert n % tm == 0
    row = pl.BlockSpec((tm, D_MODEL), lambda i: (i, 0))
    return pl.pallas_call(
        _memkv_kernel,
        grid=(n // tm,),
        in_specs=[row, pl.BlockSpec((1, D_MODEL), lambda i: (0, 0)),
                  _layer_weight(wk_b, li), _layer_weight(wv_b, li)],
        out_specs=[row, row],
        out_shape=[jax.ShapeDtypeStruct((n, D_MODEL), F32)] * 2,
        compiler_params=_cparams(1),
        name="memkv",
    )(mem2d, g.reshape(1, D_MODEL), wk_b, wv_b)


def _mix_mem_kernel(x_ref, pool_ref, sb_ref, df_ref, wout_ref, gpost0_ref, gpre1_ref, gpost1_ref,
                    wq_ref, mk_ref, mv_ref, wo_ref, o_ref, att_ref, mkb_ref, mvb_ref, *, sub):
    @pl.when(pl.program_id(1) == 0)
    def _():
        for h in range(MEM_HEADS):
            sl = slice(h * MEM_HEAD_DIM, (h + 1) * MEM_HEAD_DIM)
            mkb_ref[:, sl] = mk_ref[:, h, :].astype(BF16)
            mvb_ref[:, sl] = mv_ref[:, h, :].astype(BF16)

    for r in range(0, x_ref.shape[1], sub):
        rows = slice(r, r + sub)
        mixed = (_dot(pool_ref[0, rows, :], wout_ref[0:POOL_DIM, :])
                 + _dot(sb_ref[0, rows, :], wout_ref[POOL_DIM:POOL_DIM + SB_DIM, :])
                 + _dot(df_ref[0, rows, :], wout_ref[POOL_DIM + SB_DIM:, :]))
        x1 = x_ref[0, rows, :] + _rms(mixed, gpost0_ref[...])

        hb = _rms(x1, gpre1_ref[...]).astype(BF16)
        qm = (_dot(hb, wq_ref[...]) * (MEM_HEAD_DIM ** -0.5)).astype(BF16)
        for h in range(MEM_HEADS):
            sl = slice(h * MEM_HEAD_DIM, (h + 1) * MEM_HEAD_DIM)
            s = _dot_nt(qm[:, sl], mkb_ref[:, sl])
            e = jnp.exp(s - jnp.max(s, axis=-1, keepdims=True))
            p = e / jnp.sum(e, axis=-1, keepdims=True)
            att_ref[rows, sl] = _dot(p.astype(BF16), mvb_ref[:, sl]).astype(BF16)
        y = _dot(att_ref[rows, :], wo_ref[...])
        o_ref[0, rows, :] = x1 + _rms(y, gpost1_ref[...])


def _mix_mem(x, pool_o, sb_o, df_o, wout_b, g_post0, g_pre1, g_post1, wq_b, mem, wo_b, li, tm):
    b, t, _ = x.shape
    assert t % tm == 0
    mk, mv = mem[0], mem[1]
    m = mk.shape[-3]
    tok = lambda w: pl.BlockSpec((1, tm, w), lambda bi, ti: (bi, ti, 0))
    vec = pl.BlockSpec((1, D_MODEL), lambda bi, ti: (0, 0))
    if len(mem) == 2:
        memspec = pl.BlockSpec((None, m, MEM_HEADS, MEM_HEAD_DIM), lambda bi, ti: (bi, 0, 0, 0))
    else:
        mem_li = mem[2]
        memspec = pl.BlockSpec((None, None, m, MEM_HEADS, MEM_HEAD_DIM),
                               lambda bi, ti: (mem_li, bi, 0, 0, 0))
    return pl.pallas_call(
        functools.partial(_mix_mem_kernel, sub=min(tm, 256)),
        grid=(b, t // tm),
        in_specs=[tok(D_MODEL), tok(POOL_DIM), tok(SB_DIM), tok(DIFF_DIM),
                  _layer_weight(wout_b, li), vec, vec, vec,
                  _layer_weight(wq_b, li), memspec, memspec, _layer_weight(wo_b, li)],
        out_specs=tok(D_MODEL),
        out_shape=jax.ShapeDtypeStruct((b, t, D_MODEL), F32),
        scratch_shapes=[pltpu.VMEM((tm, D_MODEL), BF16),
                        pltpu.VMEM((m, D_MODEL), BF16), pltpu.VMEM((m, D_MODEL), BF16)],
        compiler_params=pltpu.CompilerParams(
            dimension_semantics=("parallel", "arbitrary"), vmem_limit_bytes=VMEM_LIMIT_BYTES),
        name="mix_mem",
    )(x, pool_o, sb_o, df_o, wout_b, g_post0.reshape(1, -1), g_pre1.reshape(1, -1),
      g_post1.reshape(1, -1), wq_b, mk, mv, wo_b)


def _ffn_kernel(x_ref, gpre_ref, gpost_ref, wg_ref, wu_ref, wd_ref, o_ref, *, fc, sub):
    for r in range(0, x_ref.shape[0], sub):
        x = x_ref[r:r + sub, :]
        hb = _rms(x, gpre_ref[...]).astype(BF16)
        acc = jnp.zeros(x.shape, F32)
        for c in range(D_FF // fc):
            sl = slice(c * fc, (c + 1) * fc)
            gate = _dot(hb, wg_ref[:, sl])
            up = _dot(hb, wu_ref[:, sl])
            act = (gate / (1.0 + jnp.exp(-gate))) * up
            acc = acc + _dot(act.astype(BF16), wd_ref[sl, :])
        o_ref[r:r + sub, :] = x + _rms(acc, gpost_ref[...])


def _ffn(x2d, g_pre2, g_post2, wg_b, wu_b, wd_b, li, tm, fc=256):
    n = x2d.shape[0]
    assert n % tm == 0 and D_FF % fc == 0
    row = pl.BlockSpec((tm, D_MODEL), lambda i: (i, 0))
    vec = pl.BlockSpec((1, D_MODEL), lambda i: (0, 0))
    return pl.pallas_call(
        functools.partial(_ffn_kernel, fc=fc, sub=min(tm, 256)),
        grid=(n // tm,),
        in_specs=[row, vec, vec, _layer_weight(wg_b, li), _layer_weight(wu_b, li),
                  _layer_weight(wd_b, li)],
        out_specs=row,
        out_shape=jax.ShapeDtypeStruct((n, D_MODEL), F32),
        compiler_params=_cparams(1),
        name="ffn",
    )(x2d, g_pre2.reshape(1, -1), g_post2.reshape(1, -1), wg_b, wu_b, wd_b)


def _pick(n, pref):
    return pref if n % pref == 0 else n


def _layer(x, pool_buf, sb_cache, diff_cache, mem, lw, li, lam_init, prev_leaves):
    (g_pre, g_post, w_in_b, w_out_b, wbd_b, pool_scale, lam_params, diff_g,
     wq_b, wo_b, wg_b, wu_b, wd_b) = lw
    b, t, _ = x.shape
    n = b * t
    past = 0 if sb_cache is None else sb_cache[0].shape[3]

    (u, sk, sv, dk, dv, sqb, skb, svb, dqb, dkb, dvb) = _inproj(
        x, g_pre[0], w_in_b, li, _pick(n, 512), prev_leaves)
    r3 = lambda a: a.reshape(b, t, a.shape[-1])
    u, sqb, skb, svb, dqb, dkb, dvb = map(r3, (u, sqb, skb, svb, dqb, dkb, dvb))

    buf16 = jnp.concatenate([jnp.zeros((b, 1, POOL_DIM), F32), pool_buf], axis=1)
    pool_o = _pool(u, buf16, wbd_b, pool_scale, _pick(t, 512), past)
    new_pool = jnp.concatenate([pool_buf, u], axis=1)[:, -POOL_STATE:]

    if past:
        sb_o = _sb_attention(sqb, skb, svb, sb_cache, t, _pick(past, 256))
        df_o = _diff_attention(dqb, dkb, dvb, diff_cache, lam_params, diff_g,
                               t, _pick(past, 512), lam_init)
    else:
        sb_tq = _pick(t, 256)
        sb_o = _sb_attention(sqb, skb, svb, None, sb_tq, sb_tq)
        df_tq = _pick(t, 1024)
        df_o = _diff_attention(dqb, dkb, dvb, None, lam_params, diff_g, df_tq, df_tq, lam_init)

    x = _mix_mem(x, pool_o, sb_o, df_o, w_out_b, g_post[0], g_pre[1], g_post[1],
                 wq_b, mem, wo_b, li, _pick(t, 512))
    x = _ffn(x.reshape(n, D_MODEL), g_pre[2], g_post[2], wg_b, wu_b, wd_b, li,
             _pick(n, 512)).reshape(b, t, D_MODEL)

    return x, (sk, sv, dk, dv), new_pool


def _finish_leaves(leaves, b, t):
    sk, sv, dk, dv = leaves
    nl = sk.shape[0]
    if sk.ndim == 4:
        sb = lambda a: a.reshape(nl, b, SB_HEADS, SB_HEAD_DIM, t).transpose(0, 1, 4, 2, 3)
    else:
        sb = lambda a: a.reshape(nl, b, t, SB_HEADS, SB_HEAD_DIM)
    df = lambda a: a.reshape(nl, b, t, DIFF_HEADS, DIFF_HEAD_DIM)
    return sb(sk), sb(sv), df(dk), df(dv)


def _block_diag(pool_w):
    g, c, d = pool_w.shape
    out = jnp.zeros((g * c, g * d), pool_w.dtype)
    for i in range(g):
        out = lax.dynamic_update_slice(out, pool_w[i], (i * c, i * d))
    return out


def kernel(x_prompt, x_sample, cache_sb_k, cache_sb_v, cache_diff_k, cache_diff_v, cache_mem_k, cache_mem_v, state_pool, mem_prompt, g_pre, g_post, g_mem, w_in, w_out, pool_w, pool_scale, lam_q1, lam_k1, lam_q2, lam_k2, diff_g, wq_m, wk_m, wv_m, wo_m, w_gate, w_up, w_down):
    depth = w_in.shape[0]
    xp, xs = x_prompt, x_sample
    bp, mem_len = mem_prompt.shape[0], mem_prompt.shape[1]
    dshape = cache_diff_k.shape
    diff_k_rows = cache_diff_k.reshape(dshape[0], dshape[1], dshape[2] * dshape[3], dshape[4])
    diff_v_rows = cache_diff_v.reshape(dshape[0], dshape[1], dshape[2] * dshape[3], dshape[4])
    sshape = cache_sb_k.shape
    to_feature_major = lambda c: c.transpose(0, 1, 3, 4, 2).reshape(
        sshape[0], sshape[1], sshape[3] * sshape[4], sshape[2])
    sb_kT, sb_vT = to_feature_major(cache_sb_k), to_feature_major(cache_sb_v)
    (w_in_b, w_out_b, wq_b, wk_b, wv_b, wo_b, wg_b, wu_b, wd_b) = [
        w.astype(BF16) for w in (w_in, w_out, wq_m, wk_m, wv_m, wo_m, w_gate, w_up, w_down)]
    p_pool, p_mk, p_mv, s_pool = [], [], [], []
    p_leaves = s_leaves = None
    for li in range(depth):
        lam_init = 0.8 - 0.6 * math.exp(-0.3 * li)
        lw = (g_pre[li], g_post[li], w_in_b, w_out_b, _block_diag(pool_w[li]).astype(BF16),
              pool_scale[li], (lam_q1[li], lam_k1[li], lam_q2[li], lam_k2[li]), diff_g[li],
              wq_b, wo_b, wg_b, wu_b, wd_b)
        mk2d, mv2d = _memkv(mem_prompt.reshape(bp * mem_len, D_MODEL), g_mem[li],
                            wk_b, wv_b, li, _pick(bp * mem_len, 256))
        mk = mk2d.reshape(bp, mem_len, MEM_HEADS, MEM_HEAD_DIM)
        mv = mv2d.reshape(bp, mem_len, MEM_HEADS, MEM_HEAD_DIM)
        zero_buf = jnp.zeros((xp.shape[0], POOL_STATE, POOL_DIM), xp.dtype)
        xp, p_leaves, npool = _layer(xp, zero_buf, None, None, (mk, mv), lw, li, lam_init,
                                     p_leaves)
        p_pool.append(npool); p_mk.append(mk); p_mv.append(mv)
        xs, s_leaves, npool = _layer(xs, state_pool[li], (sb_kT, sb_vT, li),
                                     (diff_k_rows, diff_v_rows, li),
                                     (cache_mem_k, cache_mem_v, li), lw, li, lam_init, s_leaves)
        s_pool.append(npool)
    return (xp, xs,
            *_finish_leaves(p_leaves, xp.shape[0], xp.shape[1]),
            jnp.stack(p_pool), jnp.stack(p_mk), jnp.stack(p_mv),
            *_finish_leaves(s_leaves, xs.shape[0], xs.shape[1]),
            jnp.stack(s_pool))
```

```python
import functools
import math

import jax
import jax.numpy as jnp
from jax import lax
from jax.experimental import pallas as pl
from jax.experimental.pallas import tpu as pltpu

F32 = jnp.float32
BF16 = jnp.bfloat16

D_MODEL = 1024
CHUNK = 64
POOL_WINDOWS = (2, 4, 8, 16)
POOL_GROUP_DIM = 64
POOL_DIM = 256
POOL_STATE = 15
SB_HEADS = 4
SB_HEAD_DIM = 64
SB_DIM = 256
DIFF_HEADS = 4
DIFF_HALF_DIM = 64
DIFF_HEAD_DIM = 128
DIFF_DIM = 512
MEM_HEADS = 4
MEM_HEAD_DIM = 256
D_FF = 2816
EPS = 1e-6
NEG_INF = -1e30
LOG2E = 1.4426950408889634

LANES = 128
MXU_DIM = 256
VMEM_LIMIT_BYTES = 56 * 1024 * 1024

SB_ZERO_LOG = 110.0


def _cparams(n_grid):
    return pltpu.CompilerParams(
        dimension_semantics=("parallel",) * n_grid,
        vmem_limit_bytes=VMEM_LIMIT_BYTES,
    )


def _rms(x, g):
    return x * lax.rsqrt(jnp.mean(x * x, axis=-1, keepdims=True) + EPS) * g


def _dot(a, b):
    return jnp.dot(a, b, preferred_element_type=F32)


def _dot_nt(a, b):
    return lax.dot_general(a, b, (((1,), (1,)), ((), ())), preferred_element_type=F32)


def _pool_tile(u, xs_ref, pos0, wbd, scale):
    tm = u.shape[0]
    xs_ref[16:, :] = u
    pos = pos0 + lax.broadcasted_iota(jnp.int32, (tm, POOL_DIM), 0)
    group = lax.broadcasted_iota(jnp.int32, (tm, POOL_DIM), 1) // POOL_GROUP_DIM
    mean = jnp.zeros((tm, POOL_DIM), F32)
    run = u
    gi = 0
    for back in range(1, max(POOL_WINDOWS)):
        run = run + xs_ref[16 - back:16 - back + tm, :]
        if back + 1 == POOL_WINDOWS[gi]:
            cnt = jnp.minimum(back + 1, pos + 1).astype(F32)
            mean = jnp.where(group == gi, run / cnt, mean)
            gi += 1
    d = (mean - u).astype(BF16)
    return (_dot(d, wbd) * scale).astype(BF16)


def _inproj_kernel(x_ref, g_ref, w_ref,
                   u_ref, sk_ref, sv_ref, dk_ref, dv_ref,
                   sqb_ref, skb_ref, svb_ref, dqb_ref, dkb_ref, dvb_ref, *, transpose_kv):
    hb = _rms(x_ref[...], g_ref[...]).astype(BF16)

    def cols(lo, hi):
        return _dot(hb, w_ref[:, lo:hi])

    u_ref[...] = cols(0, 256)
    sqb_ref[...] = (cols(256, 512) * (LOG2E * SB_HEAD_DIM ** -0.5)).astype(BF16)
    sk = cols(512, 768)
    skb_ref[...] = sk.astype(BF16)
    sv = cols(768, 1024)
    svb_ref[...] = sv.astype(BF16)
    if transpose_kv:
        sk_ref[0, 0] = sk.T
        sv_ref[0, 0] = sv.T
    else:
        sk_ref[0] = sk
        sv_ref[0] = sv
    dqb_ref[...] = (cols(1024, 1536) * (LOG2E * DIFF_HALF_DIM ** -0.5)).astype(BF16)
    tm = x_ref.shape[0]
    dk = cols(1536, 2048)
    dkb_ref[...] = dk.astype(BF16)
    dv = cols(2048, 2560)
    dvb_ref[...] = dv.astype(BF16)
    for h in range(DIFF_HEADS):
        sl = slice(h * DIFF_HEAD_DIM, (h + 1) * DIFF_HEAD_DIM)
        dk_ref[0, pl.ds(h, tm, stride=DIFF_HEADS), :] = dk[:, sl]
        dv_ref[0, pl.ds(h, tm, stride=DIFF_HEADS), :] = dv[:, sl]


def _layer_weight(w, li):
    return pl.BlockSpec((None,) + w.shape[1:], lambda *_: (li, 0, 0),
                        pipeline_mode=pl.Buffered(1))


LEAF_KINDS = ("sb", "sb", "diff", "diff")


def _leaf_stack(kind, layers, b, t, tm):
    n = b * t
    if kind == "diff":
        return ((layers, n * DIFF_HEADS, DIFF_HEAD_DIM),
                pl.BlockSpec((layers, tm * DIFF_HEADS, DIFF_HEAD_DIM), lambda i: (0, i, 0)))
    if t % tm == 0 and tm % LANES == 0:
        per_batch = t // tm
        return ((layers, b, SB_DIM, t),
                pl.BlockSpec((layers, 1, SB_DIM, tm),
                             lambda i: (0, i // per_batch, 0, i % per_batch)))
    return (layers, n, SB_DIM), pl.BlockSpec((layers, tm, SB_DIM), lambda i: (0, i, 0))


def _inproj(x, g, w_in_b, li, tm):
    b, t, _ = x.shape
    n = b * t
    assert n % tm == 0
    transpose_kv = t % tm == 0 and tm % LANES == 0
    widths = (256, 256, 256, 512, 512, 256, 256, 256, 512, 512, 512)
    dtypes = (F32,) * 5 + (BF16,) * 6
    out_shape = [jax.ShapeDtypeStruct((n, w), dt) for w, dt in zip(widths, dtypes)]
    out_specs = [pl.BlockSpec((tm, w), lambda i: (i, 0)) for w in widths]
    for i, kind in zip((1, 2, 3, 4), LEAF_KINDS):
        shape, out_specs[i] = _leaf_stack(kind, 1, b, t, tm)
        out_shape[i] = jax.ShapeDtypeStruct(shape, F32)
    return pl.pallas_call(
        functools.partial(_inproj_kernel, transpose_kv=transpose_kv),
        grid=(n // tm,),
        in_specs=[
            pl.BlockSpec((tm, D_MODEL), lambda i: (i, 0)),
            pl.BlockSpec((1, D_MODEL), lambda i: (0, 0)),
            _layer_weight(w_in_b, li),
        ],
        out_specs=out_specs,
        out_shape=out_shape,
        compiler_params=_cparams(1),
        name="inproj",
    )(x.reshape(n, D_MODEL), g.reshape(1, D_MODEL), w_in_b)


def _pool_kernel(u_ref, prev_ref, buf_ref, wbd_ref, scale_ref, o_ref, xs_ref, *, tm, start_pos):
    t = pl.program_id(1)
    xs_ref[0:16, :] = jnp.where(t == 0, buf_ref[0], prev_ref[0])
    o_ref[0] = _pool_tile(u_ref[0], xs_ref, start_pos + t * tm, wbd_ref[...], scale_ref[...])


def _pool(u, buf16, wbd_b, pool_scale, tm, start_pos):
    b, t, _ = u.shape
    assert t % tm == 0 and tm % 16 == 0
    r = tm // 16
    return pl.pallas_call(
        functools.partial(_pool_kernel, tm=tm, start_pos=start_pos),
        grid=(b, t // tm),
        in_specs=[
            pl.BlockSpec((1, tm, POOL_DIM), lambda bi, ti: (bi, ti, 0)),
            pl.BlockSpec((1, 16, POOL_DIM), lambda bi, ti: (bi, jnp.maximum(ti * r - 1, 0), 0)),
            pl.BlockSpec((1, 16, POOL_DIM), lambda bi, ti: (bi, 0, 0)),
            pl.BlockSpec((POOL_DIM, POOL_DIM), lambda bi, ti: (0, 0)),
            pl.BlockSpec((1, POOL_DIM), lambda bi, ti: (0, 0)),
        ],
        out_specs=pl.BlockSpec((1, tm, POOL_DIM), lambda bi, ti: (bi, ti, 0)),
        out_shape=jax.ShapeDtypeStruct((b, t, POOL_DIM), BF16),
        scratch_shapes=[pltpu.VMEM((tm + 16, POOL_DIM), F32)],
        compiler_params=_cparams(2),
        name="pool",
    )(u, u, buf16, wbd_b, pool_scale.reshape(1, POOL_DIM))


def _suffix_ones(n):
    r = lax.broadcasted_iota(jnp.int32, (n, n), 0)
    c = lax.broadcasted_iota(jnp.int32, (n, n), 1)
    return jnp.where(r >= c, 1.0, 0.0).astype(BF16)


def _sb_kernel(q_ref, kd_ref, vd_ref, kp_ref, vp_ref, o_ref, acc_ref, ca_ref, cb_ref,
               *, tq, tk, past_blocks, past_transposed):
    qi = pl.program_id(2)
    lane = lax.broadcasted_iota(jnp.int32, (tq, LANES), 1)
    first = lane < SB_HEAD_DIM
    qf = q_ref[0].astype(F32)
    qa = jnp.where(first, qf, 0.0).astype(BF16)
    qb = jnp.where(first, 0.0, qf).astype(BF16)

    def head(qh, k, mask, ones, transposed):
        z = _dot(qh, k) if transposed else _dot_nt(qh, k)
        sp = jnp.maximum(z, 0.0) + jnp.log2(1.0 + jnp.exp2(-jnp.abs(z)))
        if mask is not None:
            sp = jnp.where(mask, sp, 0.0)
        hi = sp.astype(BF16)
        lo = (sp - hi.astype(F32)).astype(BF16)
        return z, _dot(hi, ones) + _dot(lo, ones)

    def weigh(za, csa, zb, csb, ca, cb, v, mask, transposed):
        wa = jnp.exp2(za - csa - ca)
        wb = jnp.exp2(zb - csb - cb)
        if mask is not None:
            wa = jnp.where(mask, wa, 0.0)
            wb = jnp.where(mask, wb, 0.0)
        apply = _dot_nt if transposed else _dot
        pv = jnp.where(first, apply(wa.astype(BF16), v), apply(wb.astype(BF16), v))
        return pv, ca + csa[:, 0:1], cb + csb[:, 0:1]

    def past(j):
        start = pl.multiple_of(jnp.maximum(j, 0) * tk, tk)
        if past_transposed:
            return (kp_ref[:, pl.ds(start, tk)].astype(BF16),
                    vp_ref[:, pl.ds(start, tk)].astype(BF16))
        return kp_ref[pl.ds(start, tk), :].astype(BF16), vp_ref[pl.ds(start, tk), :].astype(BF16)

    n_past = past_blocks(qi)
    ones_p = _suffix_ones(tk)
    row = lax.broadcasted_iota(jnp.int32, (tq, tq), 0)
    col = lax.broadcasted_iota(jnp.int32, (tq, tq), 1)
    own = col < row
    have_prev = n_past > 0
    kd, vd = kd_ref[0], vd_ref[0]
    k1, v1 = past(n_past - 1)
    za, csa = head(qa, kd, own, _suffix_ones(tq), False)
    zb, csb = head(qb, kd, own, _suffix_ones(tq), False)
    za1, csa1 = head(qa, k1, have_prev, ones_p, past_transposed)
    zb1, csb1 = head(qb, k1, have_prev, ones_p, past_transposed)
    pv0, ca, cb = weigh(za, csa, zb, csb, 0.0, 0.0, vd, own, False)
    pv1, ca, cb = weigh(za1, csa1, zb1, csb1, ca, cb, v1, have_prev, past_transposed)
    acc_ref[...] = pv0 + pv1
    ca_ref[...] = ca
    cb_ref[...] = cb

    def cond(state):
        j, cm = state
        return jnp.logical_and(j >= 0, cm <= SB_ZERO_LOG * LOG2E)

    def body(state):
        j, _ = state
        k, v = past(j)
        za, csa = head(qa, k, None, ones_p, past_transposed)
        zb, csb = head(qb, k, None, ones_p, past_transposed)
        pv, ca, cb = weigh(za, csa, zb, csb, ca_ref[...], cb_ref[...], v, None, past_transposed)
        acc_ref[...] += pv
        ca_ref[...] = ca
        cb_ref[...] = cb
        return j - 1, jnp.min(jnp.minimum(ca, cb))

    lax.while_loop(cond, body, (n_past - 2, jnp.min(jnp.minimum(ca, cb))))
    o_ref[0] = acc_ref[...].astype(BF16)


def _sb_attention(qb, kb, vb, cache, tq, tk):
    b, t, _ = qb.shape
    tile = pl.BlockSpec((1, tq, LANES), lambda bi, hi, qi: (bi, qi, hi))
    if cache is None:
        assert t % tq == 0 and tq % tk == 0
        past_blocks = lambda qi: qi * (tq // tk)
        full = pl.BlockSpec((None, t, LANES), lambda bi, hi, qi: (bi, 0, hi))
        k_past, v_past = kb, vb
    else:
        k_past, v_past, li = cache
        p = k_past.shape[3]
        assert t == tq and p % tk == 0
        past_blocks = lambda qi: jnp.int32(p // tk)
        full = pl.BlockSpec((None, None, LANES, p), lambda bi, hi, qi: (li, bi, hi, 0))
    return pl.pallas_call(
        functools.partial(_sb_kernel, tq=tq, tk=tk, past_blocks=past_blocks,
                          past_transposed=cache is not None),
        grid=(b, SB_DIM // LANES, t // tq),
        in_specs=[tile, tile, tile, full, full],
        out_specs=tile,
        out_shape=jax.ShapeDtypeStruct((b, t, SB_DIM), BF16),
        scratch_shapes=[
            pltpu.VMEM((tq, LANES), F32),
            pltpu.VMEM((tq, 1), F32),
            pltpu.VMEM((tq, 1), F32),
        ],
        compiler_params=_cparams(3),
        name="sb_attn",
    )(qb, kb, vb, k_past, v_past)


def _exp2_rows(s, m):
    n = s.shape[1] // LANES
    if n == 0:
        return jnp.exp2(s - m[:, :s.shape[1]]).astype(BF16)
    return jnp.concatenate(
        [jnp.exp2(s[:, c * LANES:(c + 1) * LANES] - m).astype(BF16) for c in range(n)], axis=1)


def _diff_kernel(*refs, tq, tk, row_chunk, head_stride, diag_in_past, past_blocks, q_pos0,
                 lam_init):
    if diag_in_past:
        q_ref, kp_ref, vp_ref = refs[:3]
        kd_ref = vd_ref = None
        rest = refs[3:]
    else:
        q_ref, kd_ref, vd_ref, kp_ref, vp_ref = refs[:5]
        rest = refs[5:]
    (lq1_ref, lk1_ref, lq2_ref, lk2_ref, g_ref, o_ref, m_ref, a_ref, p_ref, alpha_ref) = rest
    qi = pl.program_id(2)
    lane = lax.broadcasted_iota(jnp.int32, (tq, LANES), 1)
    first = lane < DIFF_HALF_DIM
    qf = q_ref[0].astype(F32)
    q2 = jnp.concatenate([jnp.where(first, qf, 0.0), jnp.where(first, 0.0, qf)],
                         axis=0).astype(BF16)
    n_max = kp_ref.shape[0] // (tk * head_stride)
    rows = min(2 * tq, row_chunk)

    def with_ones(v):
        return jnp.concatenate([v, jnp.ones(v.shape, BF16)], axis=1)

    def past(ref, j):
        start = jnp.clip(j, 0, n_max - 1) * tk
        if head_stride == 1:
            return ref[pl.ds(pl.multiple_of(start, tk), tk), :].astype(BF16)
        first_row = start * head_stride + pl.program_id(1)
        return ref[pl.ds(first_row, tk, stride=head_stride), :].astype(BF16)

    def own_mask(r, n_rows, n_keys):
        row = (r + lax.broadcasted_iota(jnp.int32, (n_rows, n_keys), 0)) % tq
        qpos = q_pos0(qi) + row
        kpos = q_pos0(qi) + lax.broadcasted_iota(jnp.int32, (n_rows, n_keys), 1)
        return (kpos // CHUNK) <= (qpos // CHUNK)

    def step(k, v1, own, pending_own=False):
        for r in range(0, 2 * tq, rows):
            sl = slice(r, r + rows)
            if v1 is not None:
                nk = r % tq + rows if pending_own else tk
                alpha = alpha_ref[sl]
                a_ref[sl] = (jnp.concatenate([alpha, alpha], axis=1) * a_ref[sl]
                             + _dot(p_ref[sl, :nk], v1[:nk]))
            if k is not None:
                nk = r % tq + rows if own else tk
                s = _dot_nt(q2[sl], k[:nk])
                if own:
                    s = jnp.where(own_mask(r, rows, nk), s, NEG_INF)
                m_old = m_ref[sl]
                m_new = jnp.maximum(m_old, jnp.max(s, axis=-1, keepdims=True))
                alpha_ref[sl] = jnp.exp2(m_old - m_new)
                m_ref[sl] = m_new
                p_ref[sl, :nk] = _exp2_rows(s, m_new)

    n_past = past_blocks(qi)
    first_step = (pl.program_id(0) == 0) & (pl.program_id(1) == 0) & (qi == 0)
    if diag_in_past:
        m_ref[...] = jnp.full(m_ref.shape, NEG_INF, F32)

        @pl.when(first_step)
        def _():
            a_ref[...] = jnp.zeros_like(a_ref)
            alpha_ref[...] = jnp.ones_like(alpha_ref)
    else:
        s = jnp.where(own_mask(0, 2 * tq, tq), _dot_nt(q2, kd_ref[0]), NEG_INF)
        m = jnp.broadcast_to(jnp.max(s, axis=-1, keepdims=True), (2 * tq, LANES))
        m_ref[...] = m
        a_ref[...] = _dot(_exp2_rows(s, m), with_ones(vd_ref[0]))
        alpha_ref[...] = jnp.ones_like(alpha_ref)

    @pl.when(first_step)
    def _():
        p_ref[...] = jnp.zeros_like(p_ref)

    def pending_values(j):
        v1 = with_ones(past(vp_ref, j))
        return jnp.where(j >= 0, v1, jnp.zeros_like(v1))

    def body(j, carry):
        step(past(kp_ref, j), pending_values(j - 1), False)
        return carry

    lax.fori_loop(0, n_past, body, 0)
    if diag_in_past:
        step(past(kp_ref, n_past), pending_values(n_past - 1), True)
        step(None, with_ones(past(vp_ref, n_past)), False, pending_own=True)
    else:
        step(None, with_ones(past(vp_ref, n_past - 1)), False)

    lam = (jnp.exp(jnp.sum(lq1_ref[...] * lk1_ref[...], axis=-1, keepdims=True))
           - jnp.exp(jnp.sum(lq2_ref[...] * lk2_ref[...], axis=-1, keepdims=True)) + lam_init)
    o = (a_ref[:tq, :LANES] / a_ref[:tq, LANES:]
         - lam * (a_ref[tq:, :LANES] / a_ref[tq:, LANES:]))
    o_ref[0] = (_rms(o, g_ref[...]) * (1.0 - lam_init)).astype(BF16)


def _diff_attention(qb, kb, vb, cache, lam_params, diff_g, tq, tk, lam_init):
    b, t, _ = qb.shape
    tile = pl.BlockSpec((1, tq, LANES), lambda bi, hi, qi: (bi, qi, hi))
    vec64 = pl.BlockSpec((1, DIFF_HALF_DIM), lambda bi, hi, qi: (0, 0))
    vec128 = pl.BlockSpec((1, DIFF_HEAD_DIM), lambda bi, hi, qi: (0, 0))
    if cache is None:
        assert tq == tk and t % tq == 0
        past_len, head_stride = 0, 1
        past_blocks = lambda qi: qi
        full = pl.BlockSpec((None, t, LANES), lambda bi, hi, qi: (bi, 0, hi))
        kv_specs, kv_args = [full, full], (kb, vb)
    else:
        k_cache, v_cache, li = cache
        rows = k_cache.shape[2]
        past_len, head_stride = rows // DIFF_HEADS, DIFF_HEADS
        assert t == tq and past_len % tk == 0
        past_blocks = lambda qi: past_len // tk
        full = pl.BlockSpec((None, None, rows, LANES), lambda bi, hi, qi: (li, bi, 0, 0))
        kv_specs, kv_args = [tile, tile, full, full], (kb, vb, k_cache, v_cache)
    q_pos0 = lambda qi: past_len + qi * tq
    return pl.pallas_call(
        functools.partial(_diff_kernel, tq=tq, tk=tk, row_chunk=128, head_stride=head_stride,
                          diag_in_past=cache is None, past_blocks=past_blocks, q_pos0=q_pos0,
                          lam_init=lam_init),
        grid=(b, DIFF_HEADS, t // tq),
        in_specs=[tile] + kv_specs + [vec64, vec64, vec64, vec64, vec128],
        out_specs=tile,
        out_shape=jax.ShapeDtypeStruct((b, t, DIFF_DIM), BF16),
        scratch_shapes=[
            pltpu.VMEM((2 * tq, LANES), F32), pltpu.VMEM((2 * tq, 2 * LANES), F32),
            pltpu.VMEM((2 * tq, tk), BF16), pltpu.VMEM((2 * tq, LANES), F32),
        ],
        compiler_params=pltpu.CompilerParams(
            dimension_semantics=("arbitrary",) * 3, vmem_limit_bytes=VMEM_LIMIT_BYTES),
        name="diff_attn",
    )(qb, *kv_args,
      *[a.reshape(1, DIFF_HALF_DIM) for a in lam_params], diff_g.reshape(1, DIFF_HEAD_DIM))


def _memkv_kernel(m_ref, g_ref, wk_ref, wv_ref, k_ref, v_ref):
    hb = _rms(m_ref[...], g_ref[...]).astype(BF16)
    k_ref[...] = _dot(hb, wk_ref[...])
    v_ref[...] = _dot(hb, wv_ref[...])


def _memkv(mem2d, g, wk_b, wv_b, li, tm):
    n = mem2d.shape[0]
    assert n % tm == 0
    row = pl.BlockSpec((tm, D_MODEL), lambda i: (i, 0))
    return pl.pallas_call(
        _memkv_kernel,
        grid=(n // tm,),
        in_specs=[row, pl.BlockSpec((1, D_MODEL), lambda i: (0, 0)),
                  _layer_weight(wk_b, li), _layer_weight(wv_b, li)],
        out_specs=[row, row],
        out_shape=[jax.ShapeDtypeStruct((n, D_MODEL), F32)] * 2,
        compiler_params=_cparams(1),
        name="memkv",
    )(mem2d, g.reshape(1, D_MODEL), wk_b, wv_b)


def _mix_mem_kernel(x_ref, pool_ref, sb_ref, df_ref, wout_ref, gpost0_ref, gpre1_ref, gpost1_ref,
                    wq_ref, mk_ref, mv_ref, wo_ref, o_ref, att_ref, mkb_ref, mvb_ref, *, sub):
    @pl.when(pl.program_id(1) == 0)
    def _():
        for h in range(MEM_HEADS):
            sl = slice(h * MEM_HEAD_DIM, (h + 1) * MEM_HEAD_DIM)
            mkb_ref[:, sl] = mk_ref[:, h, :].astype(BF16)
            mvb_ref[:, sl] = mv_ref[:, h, :].astype(BF16)

    for r in range(0, x_ref.shape[1], sub):
        rows = slice(r, r + sub)
        mixed = (_dot(pool_ref[0, rows, :], wout_ref[0:POOL_DIM, :])
                 + _dot(sb_ref[0, rows, :], wout_ref[POOL_DIM:POOL_DIM + SB_DIM, :])
                 + _dot(df_ref[0, rows, :], wout_ref[POOL_DIM + SB_DIM:, :]))
        x1 = x_ref[0, rows, :] + _rms(mixed, gpost0_ref[...])

        hb = _rms(x1, gpre1_ref[...]).astype(BF16)
        qm = (_dot(hb, wq_ref[...]) * (MEM_HEAD_DIM ** -0.5)).astype(BF16)
        for h in range(MEM_HEADS):
            sl = slice(h * MEM_HEAD_DIM, (h + 1) * MEM_HEAD_DIM)
            s = _dot_nt(qm[:, sl], mkb_ref[:, sl])
            e = jnp.exp(s - jnp.max(s, axis=-1, keepdims=True))
            p = e / jnp.sum(e, axis=-1, keepdims=True)
            att_ref[rows, sl] = _dot(p.astype(BF16), mvb_ref[:, sl]).astype(BF16)
        y = _dot(att_ref[rows, :], wo_ref[...])
        o_ref[0, rows, :] = x1 + _rms(y, gpost1_ref[...])


def _mix_mem(x, pool_o, sb_o, df_o, wout_b, g_post0, g_pre1, g_post1, wq_b, mem, wo_b, li, tm):
    b, t, _ = x.shape
    assert t % tm == 0
    mk, mv = mem[0], mem[1]
    m = mk.shape[-3]
    tok = lambda w: pl.BlockSpec((1, tm, w), lambda bi, ti: (bi, ti, 0))
    vec = pl.BlockSpec((1, D_MODEL), lambda bi, ti: (0, 0))
    if len(mem) == 2:
        memspec = pl.BlockSpec((None, m, MEM_HEADS, MEM_HEAD_DIM), lambda bi, ti: (bi, 0, 0, 0))
    else:
        mem_li = mem[2]
        memspec = pl.BlockSpec((None, None, m, MEM_HEADS, MEM_HEAD_DIM),
                               lambda bi, ti: (mem_li, bi, 0, 0, 0))
    return pl.pallas_call(
        functools.partial(_mix_mem_kernel, sub=min(tm, 512)),
        grid=(b, t // tm),
        in_specs=[tok(D_MODEL), tok(POOL_DIM), tok(SB_DIM), tok(DIFF_DIM),
                  _layer_weight(wout_b, li), vec, vec, vec,
                  _layer_weight(wq_b, li), memspec, memspec, _layer_weight(wo_b, li)],
        out_specs=tok(D_MODEL),
        out_shape=jax.ShapeDtypeStruct((b, t, D_MODEL), F32),
        scratch_shapes=[pltpu.VMEM((tm, D_MODEL), BF16),
                        pltpu.VMEM((m, D_MODEL), BF16), pltpu.VMEM((m, D_MODEL), BF16)],
        compiler_params=pltpu.CompilerParams(
            dimension_semantics=("parallel", "arbitrary"), vmem_limit_bytes=VMEM_LIMIT_BYTES),
        name="mix_mem",
    )(x, pool_o, sb_o, df_o, wout_b, g_post0.reshape(1, -1), g_pre1.reshape(1, -1),
      g_post1.reshape(1, -1), wq_b, mk, mv, wo_b)


def _ffn_kernel(*refs, fc, sub, leaf_layers):
    x_ref, gpre_ref, gpost_ref, wg_ref, wu_ref, wd_ref = refs[:6]
    n_leaf = 4 * leaf_layers
    leaf_in, o_ref, leaf_out = refs[6:6 + n_leaf], refs[6 + n_leaf], refs[7 + n_leaf:]
    for layer in range(leaf_layers):
        for dst, src in zip(leaf_out, leaf_in[4 * layer:4 * layer + 4]):
            dst[layer] = src[0]
    for r in range(0, x_ref.shape[0], sub):
        x = x_ref[r:r + sub, :]
        hb = _rms(x, gpre_ref[...]).astype(BF16)
        acc = jnp.zeros(x.shape, F32)
        for c in range(D_FF // fc):
            sl = slice(c * fc, (c + 1) * fc)
            gate = _dot(hb, wg_ref[:, sl])
            up = _dot(hb, wu_ref[:, sl])
            act = (gate / (1.0 + jnp.exp(-gate))) * up
            acc = acc + _dot(act.astype(BF16), wd_ref[sl, :])
        o_ref[r:r + sub, :] = x + _rms(acc, gpost_ref[...])


def _ffn(x, g_pre2, g_post2, wg_b, wu_b, wd_b, li, tm, leaves=(), fc=D_FF):
    b, t, _ = x.shape
    n = b * t
    assert n % tm == 0 and D_FF % fc == 0
    row = pl.BlockSpec((tm, D_MODEL), lambda i: (i, 0))
    vec = pl.BlockSpec((1, D_MODEL), lambda i: (0, 0))
    leaf_in_specs = [_leaf_stack(kind, 1, b, t, tm)[1] for _ in leaves for kind in LEAF_KINDS]
    stacks = [_leaf_stack(kind, len(leaves), b, t, tm) for kind in LEAF_KINDS] if leaves else []
    outs = pl.pallas_call(
        functools.partial(_ffn_kernel, fc=fc, sub=min(tm, 256), leaf_layers=len(leaves)),
        grid=(n // tm,),
        in_specs=[row, vec, vec, _layer_weight(wg_b, li), _layer_weight(wu_b, li),
                  _layer_weight(wd_b, li)] + leaf_in_specs,
        out_specs=[row] + [spec for _, spec in stacks],
        out_shape=[jax.ShapeDtypeStruct((n, D_MODEL), F32)]
        + [jax.ShapeDtypeStruct(shape, F32) for shape, _ in stacks],
        compiler_params=_cparams(1),
        name="ffn",
    )(x.reshape(n, D_MODEL), g_pre2.reshape(1, -1), g_post2.reshape(1, -1), wg_b, wu_b, wd_b,
      *[a for layer in leaves for a in layer])
    return outs[0].reshape(b, t, D_MODEL), tuple(outs[1:])


def _pick(n, pref):
    return pref if n % pref == 0 else n


def _layer(x, pool_buf, sb_cache, diff_cache, mem, lw, li, lam_init, prev_leaves, last):
    (g_pre, g_post, w_in_b, w_out_b, wbd_b, pool_scale, lam_params, diff_g,
     wq_b, wo_b, wg_b, wu_b, wd_b) = lw
    b, t, _ = x.shape
    n = b * t
    past = 0 if sb_cache is None else sb_cache[0].shape[3]

    (u, sk, sv, dk, dv, sqb, skb, svb, dqb, dkb, dvb) = _inproj(
        x, g_pre[0], w_in_b, li, _pick(n, 512))
    r3 = lambda a: a.reshape(b, t, a.shape[-1])
    u, sqb, skb, svb, dqb, dkb, dvb = map(r3, (u, sqb, skb, svb, dqb, dkb, dvb))

    buf16 = jnp.concatenate([jnp.zeros((b, 1, POOL_DIM), F32), pool_buf], axis=1)
    pool_o = _pool(u, buf16, wbd_b, pool_scale, _pick(t, 512), past)
    new_pool = jnp.concatenate([pool_buf, u], axis=1)[:, -POOL_STATE:]

    if past:
        sb_o = _sb_attention(sqb, skb, svb, sb_cache, t, _pick(past, 256))
        df_o = _diff_attention(dqb, dkb, dvb, diff_cache, lam_params, diff_g,
                               t, _pick(past, 512), lam_init)
    else:
        sb_tq = _pick(t, 256)
        sb_o = _sb_attention(sqb, skb, svb, None, sb_tq, sb_tq)
        df_tq = _pick(t, 1024)
        df_o = _diff_attention(dqb, dkb, dvb, None, lam_params, diff_g, df_tq, df_tq, lam_init)

    x = _mix_mem(x, pool_o, sb_o, df_o, w_out_b, g_post[0], g_pre[1], g_post[1],
                 wq_b, mem, wo_b, li, _pick(t, 512))
    leaves = prev_leaves + [(sk, sv, dk, dv)]
    x, stacks = _ffn(x, g_pre[2], g_post[2], wg_b, wu_b, wd_b, li, _pick(n, 512),
                     leaves if last else ())
    return x, (stacks if last else leaves), new_pool


def _finish_leaves(leaves, b, t):
    sk, sv, dk, dv = leaves
    nl = sk.shape[0]
    if sk.ndim == 4:
        sb = lambda a: a.reshape(nl, b, SB_HEADS, SB_HEAD_DIM, t).transpose(0, 1, 4, 2, 3)
    else:
        sb = lambda a: a.reshape(nl, b, t, SB_HEADS, SB_HEAD_DIM)
    df = lambda a: a.reshape(nl, b, t, DIFF_HEADS, DIFF_HEAD_DIM)
    return sb(sk), sb(sv), df(dk), df(dv)


def _block_diag(pool_w):
    g, c, d = pool_w.shape
    out = jnp.zeros((g * c, g * d), pool_w.dtype)
    for i in range(g):
        out = lax.dynamic_update_slice(out, pool_w[i], (i * c, i * d))
    return out


def kernel(x_prompt, x_sample, cache_sb_k, cache_sb_v, cache_diff_k, cache_diff_v, cache_mem_k, cache_mem_v, state_pool, mem_prompt, g_pre, g_post, g_mem, w_in, w_out, pool_w, pool_scale, lam_q1, lam_k1, lam_q2, lam_k2, diff_g, wq_m, wk_m, wv_m, wo_m, w_gate, w_up, w_down):
    depth = w_in.shape[0]
    xp, xs = x_prompt, x_sample
    bp, mem_len = mem_prompt.shape[0], mem_prompt.shape[1]
    dshape = cache_diff_k.shape
    diff_k_rows = cache_diff_k.reshape(dshape[0], dshape[1], dshape[2] * dshape[3], dshape[4])
    diff_v_rows = cache_diff_v.reshape(dshape[0], dshape[1], dshape[2] * dshape[3], dshape[4])
    sshape = cache_sb_k.shape
    to_feature_major = lambda c: c.transpose(0, 1, 3, 4, 2).reshape(
        sshape[0], sshape[1], sshape[3] * sshape[4], sshape[2])
    sb_kT, sb_vT = to_feature_major(cache_sb_k), to_feature_major(cache_sb_v)
    (w_in_b, w_out_b, wq_b, wk_b, wv_b, wo_b, wg_b, wu_b, wd_b) = [
        w.astype(BF16) for w in (w_in, w_out, wq_m, wk_m, wv_m, wo_m, w_gate, w_up, w_down)]
    p_pool, p_mk, p_mv, s_pool = [], [], [], []
    p_leaves, s_leaves = [], []
    for li in range(depth):
        lam_init = 0.8 - 0.6 * math.exp(-0.3 * li)
        lw = (g_pre[li], g_post[li], w_in_b, w_out_b, _block_diag(pool_w[li]).astype(BF16),
              pool_scale[li], (lam_q1[li], lam_k1[li], lam_q2[li], lam_k2[li]), diff_g[li],
              wq_b, wo_b, wg_b, wu_b, wd_b)
        mk2d, mv2d = _memkv(mem_prompt.reshape(bp * mem_len, D_MODEL), g_mem[li],
                            wk_b, wv_b, li, _pick(bp * mem_len, 256))
        mk = mk2d.reshape(bp, mem_len, MEM_HEADS, MEM_HEAD_DIM)
        mv = mv2d.reshape(bp, mem_len, MEM_HEADS, MEM_HEAD_DIM)
        zero_buf = jnp.zeros((xp.shape[0], POOL_STATE, POOL_DIM), xp.dtype)
        last = li == depth - 1
        xp, p_leaves, npool = _layer(xp, zero_buf, None, None, (mk, mv), lw, li, lam_init,
                                     p_leaves, last)
        p_pool.append(npool); p_mk.append(mk); p_mv.append(mv)
        xs, s_leaves, npool = _layer(xs, state_pool[li], (sb_kT, sb_vT, li),
                                     (diff_k_rows, diff_v_rows, li),
                                     (cache_mem_k, cache_mem_v, li), lw, li, lam_init, s_leaves,
                                     last)
        s_pool.append(npool)
    return (xp, xs,
            *_finish_leaves(p_leaves, xp.shape[0], xp.shape[1]),
            jnp.stack(p_pool), jnp.stack(p_mk), jnp.stack(p_mv),
            *_finish_leaves(s_leaves, xs.shape[0], xs.shape[1]),
            jnp.stack(s_pool))
```

```python
import functools
import math

import jax
import jax.numpy as jnp
from jax import lax
from jax.experimental import pallas as pl
from jax.experimental.pallas import tpu as pltpu

F32 = jnp.float32
BF16 = jnp.bfloat16

D_MODEL = 1024
CHUNK = 64
POOL_WINDOWS = (2, 4, 8, 16)
POOL_GROUP_DIM = 64
POOL_DIM = 256
POOL_STATE = 15
SB_HEADS = 4
SB_HEAD_DIM = 64
SB_DIM = 256
DIFF_HEADS = 4
DIFF_HALF_DIM = 64
DIFF_HEAD_DIM = 128
DIFF_DIM = 512
MEM_HEADS = 4
MEM_HEAD_DIM = 256
D_FF = 2816
EPS = 1e-6
NEG_INF = -1e30
LOG2E = 1.4426950408889634

LANES = 128
MXU_DIM = 256
VMEM_LIMIT_BYTES = 56 * 1024 * 1024

SB_ZERO_LOG = 110.0


def _cparams(n_grid):
    return pltpu.CompilerParams(
        dimension_semantics=("parallel",) * n_grid,
        vmem_limit_bytes=VMEM_LIMIT_BYTES,
    )


def _rms(x, g):
    return x * lax.rsqrt(jnp.mean(x * x, axis=-1, keepdims=True) + EPS) * g


def _dot(a, b):
    return jnp.dot(a, b, preferred_element_type=F32)


def _dot_nt(a, b):
    return lax.dot_general(a, b, (((1,), (1,)), ((), ())), preferred_element_type=F32)


def _pool_tile(u, xs_ref, pos0, wbd, scale):
    tm = u.shape[0]
    xs_ref[16:, :] = u
    pos = pos0 + lax.broadcasted_iota(jnp.int32, (tm, POOL_DIM), 0)
    group = lax.broadcasted_iota(jnp.int32, (tm, POOL_DIM), 1) // POOL_GROUP_DIM
    mean = jnp.zeros((tm, POOL_DIM), F32)
    assert POOL_WINDOWS == (2, 4, 8, 16)
    run = xs_ref[...]
    for gi, w in enumerate(POOL_WINDOWS):
        run = run + pltpu.roll(run, w // 2, 0)
        cnt = jnp.minimum(w, pos + 1).astype(F32)
        mean = jnp.where(group == gi, run[16:] / cnt, mean)
    d = (mean - u).astype(BF16)
    return (_dot(d, wbd) * scale).astype(BF16)


def _inproj_kernel(x_ref, g_ref, w_ref,
                   u_ref, sk_ref, sv_ref, dk_ref, dv_ref,
                   sqb_ref, skb_ref, svb_ref, dqb_ref, dkb_ref, dvb_ref, *, transpose_kv):
    hb = _rms(x_ref[...], g_ref[...]).astype(BF16)

    def cols(lo, hi):
        return _dot(hb, w_ref[:, lo:hi])

    u_ref[...] = cols(0, 256)
    sqb_ref[...] = (cols(256, 512) * (LOG2E * SB_HEAD_DIM ** -0.5)).astype(BF16)
    sk = cols(512, 768)
    skb_ref[...] = sk.astype(BF16)
    sv = cols(768, 1024)
    svb_ref[...] = sv.astype(BF16)
    if transpose_kv:
        sk_ref[0, 0] = sk.T
        sv_ref[0, 0] = sv.T
    else:
        sk_ref[0] = sk
        sv_ref[0] = sv
    dqb_ref[...] = (cols(1024, 1536) * (LOG2E * DIFF_HALF_DIM ** -0.5)).astype(BF16)
    tm = x_ref.shape[0]
    dk = cols(1536, 2048)
    dkb_ref[...] = dk.astype(BF16)
    dv = cols(2048, 2560)
    dvb_ref[...] = dv.astype(BF16)
    for h in range(DIFF_HEADS):
        sl = slice(h * DIFF_HEAD_DIM, (h + 1) * DIFF_HEAD_DIM)
        dk_ref[0, pl.ds(h, tm, stride=DIFF_HEADS), :] = dk[:, sl]
        dv_ref[0, pl.ds(h, tm, stride=DIFF_HEADS), :] = dv[:, sl]


def _layer_weight(w, li):
    return pl.BlockSpec((None,) + w.shape[1:], lambda *_: (li, 0, 0),
                        pipeline_mode=pl.Buffered(1))


LEAF_KINDS = ("sb", "sb", "diff", "diff")


def _leaf_stack(kind, layers, b, t, tm):
    n = b * t
    if kind == "diff":
        return ((layers, n * DIFF_HEADS, DIFF_HEAD_DIM),
                pl.BlockSpec((layers, tm * DIFF_HEADS, DIFF_HEAD_DIM), lambda i: (0, i, 0)))
    if t % tm == 0 and tm % LANES == 0:
        per_batch = t // tm
        return ((layers, b, SB_DIM, t),
                pl.BlockSpec((layers, 1, SB_DIM, tm),
                             lambda i: (0, i // per_batch, 0, i % per_batch)))
    return (layers, n, SB_DIM), pl.BlockSpec((layers, tm, SB_DIM), lambda i: (0, i, 0))


def _inproj(x, g, w_in_b, li, tm):
    b, t, _ = x.shape
    n = b * t
    assert n % tm == 0
    transpose_kv = t % tm == 0 and tm % LANES == 0
    widths = (256, 256, 256, 512, 512, 256, 256, 256, 512, 512, 512)
    dtypes = (F32,) * 5 + (BF16,) * 6
    out_shape = [jax.ShapeDtypeStruct((n, w), dt) for w, dt in zip(widths, dtypes)]
    out_specs = [pl.BlockSpec((tm, w), lambda i: (i, 0)) for w in widths]
    for i, kind in zip((1, 2, 3, 4), LEAF_KINDS):
        shape, out_specs[i] = _leaf_stack(kind, 1, b, t, tm)
        out_shape[i] = jax.ShapeDtypeStruct(shape, F32)
    return pl.pallas_call(
        functools.partial(_inproj_kernel, transpose_kv=transpose_kv),
        grid=(n // tm,),
        in_specs=[
            pl.BlockSpec((tm, D_MODEL), lambda i: (i, 0)),
            pl.BlockSpec((1, D_MODEL), lambda i: (0, 0)),
            _layer_weight(w_in_b, li),
        ],
        out_specs=out_specs,
        out_shape=out_shape,
        compiler_params=_cparams(1),
        name="inproj",
    )(x.reshape(n, D_MODEL), g.reshape(1, D_MODEL), w_in_b)


def _pool_kernel(u_ref, prev_ref, buf_ref, wbd_ref, scale_ref, o_ref, xs_ref, *, tm, start_pos):
    t = pl.program_id(1)
    xs_ref[0:16, :] = jnp.where(t == 0, buf_ref[0], prev_ref[0])
    o_ref[0] = _pool_tile(u_ref[0], xs_ref, start_pos + t * tm, wbd_ref[...], scale_ref[...])


def _pool(u, buf16, wbd_b, pool_scale, tm, start_pos):
    b, t, _ = u.shape
    assert t % tm == 0 and tm % 16 == 0
    r = tm // 16
    return pl.pallas_call(
        functools.partial(_pool_kernel, tm=tm, start_pos=start_pos),
        grid=(b, t // tm),
        in_specs=[
            pl.BlockSpec((1, tm, POOL_DIM), lambda bi, ti: (bi, ti, 0)),
            pl.BlockSpec((1, 16, POOL_DIM), lambda bi, ti: (bi, jnp.maximum(ti * r - 1, 0), 0)),
            pl.BlockSpec((1, 16, POOL_DIM), lambda bi, ti: (bi, 0, 0)),
            pl.BlockSpec((POOL_DIM, POOL_DIM), lambda bi, ti: (0, 0)),
            pl.BlockSpec((1, POOL_DIM), lambda bi, ti: (0, 0)),
        ],
        out_specs=pl.BlockSpec((1, tm, POOL_DIM), lambda bi, ti: (bi, ti, 0)),
        out_shape=jax.ShapeDtypeStruct((b, t, POOL_DIM), BF16),
        scratch_shapes=[pltpu.VMEM((tm + 16, POOL_DIM), F32)],
        compiler_params=_cparams(2),
        name="pool",
    )(u, u, buf16, wbd_b, pool_scale.reshape(1, POOL_DIM))


def _suffix_ones(n):
    r = lax.broadcasted_iota(jnp.int32, (n, n), 0)
    c = lax.broadcasted_iota(jnp.int32, (n, n), 1)
    return jnp.where(r >= c, 1.0, 0.0).astype(BF16)


def _sb_kernel(q_ref, kd_ref, vd_ref, kp_ref, vp_ref, o_ref, acc_ref, ca_ref, cb_ref,
               *, tq, tk, past_blocks, past_transposed):
    qi = pl.program_id(2)
    lane = lax.broadcasted_iota(jnp.int32, (tq, LANES), 1)
    first = lane < SB_HEAD_DIM
    qf = q_ref[0].astype(F32)
    qa = jnp.where(first, qf, 0.0).astype(BF16)
    qb = jnp.where(first, 0.0, qf).astype(BF16)

    def head(qh, k, mask, ones, transposed):
        z = _dot(qh, k) if transposed else _dot_nt(qh, k)
        sp = jnp.maximum(z, 0.0) + jnp.log2(1.0 + jnp.exp2(-jnp.abs(z)))
        if mask is not None:
            sp = jnp.where(mask, sp, 0.0)
        hi = sp.astype(BF16)
        lo = (sp - hi.astype(F32)).astype(BF16)
        return z, _dot(hi, ones) + _dot(lo, ones)

    def weigh(za, csa, zb, csb, ca, cb, v, mask, transposed):
        wa = jnp.exp2(za - csa - ca)
        wb = jnp.exp2(zb - csb - cb)
        if mask is not None:
            wa = jnp.where(mask, wa, 0.0)
            wb = jnp.where(mask, wb, 0.0)
        apply = _dot_nt if transposed else _dot
        pv = jnp.where(first, apply(wa.astype(BF16), v), apply(wb.astype(BF16), v))
        return pv, ca + csa[:, 0:1], cb + csb[:, 0:1]

    def past(j):
        start = pl.multiple_of(jnp.maximum(j, 0) * tk, tk)
        if past_transposed:
            return (kp_ref[:, pl.ds(start, tk)].astype(BF16),
                    vp_ref[:, pl.ds(start, tk)].astype(BF16))
        return kp_ref[pl.ds(start, tk), :].astype(BF16), vp_ref[pl.ds(start, tk), :].astype(BF16)

    n_past = past_blocks(qi)
    ones_p = _suffix_ones(tk)
    ones_o = _suffix_ones(tq)
    row = lax.broadcasted_iota(jnp.int32, (tq, tq), 0)
    col = lax.broadcasted_iota(jnp.int32, (tq, tq), 1)
    own = col < row
    have_prev = n_past > 0
    kd, vd = kd_ref[0], vd_ref[0]
    k1, v1 = past(n_past - 1)
    za, csa = head(qa, kd, own, ones_o, False)
    zb, csb = head(qb, kd, own, ones_o, False)
    za1, csa1 = head(qa, k1, have_prev, ones_p, past_transposed)
    zb1, csb1 = head(qb, k1, have_prev, ones_p, past_transposed)
    pv0, ca, cb = weigh(za, csa, zb, csb, 0.0, 0.0, vd, own, False)
    pv1, ca, cb = weigh(za1, csa1, zb1, csb1, ca, cb, v1, have_prev, past_transposed)
    acc_ref[...] = pv0 + pv1
    ca_ref[...] = ca
    cb_ref[...] = cb

    def cond(state):
        j, cm = state
        return jnp.logical_and(j >= 0, cm <= SB_ZERO_LOG * LOG2E)

    def body(state):
        j, _ = state
        k, v = past(j)
        za, csa = head(qa, k, None, ones_p, past_transposed)
        zb, csb = head(qb, k, None, ones_p, past_transposed)
        pv, ca, cb = weigh(za, csa, zb, csb, ca_ref[...], cb_ref[...], v, None, past_transposed)
        acc_ref[...] += pv
        ca_ref[...] = ca
        cb_ref[...] = cb
        return j - 1, jnp.min(jnp.minimum(ca, cb))

    lax.while_loop(cond, body, (n_past - 2, jnp.min(jnp.minimum(ca, cb))))
    o_ref[0] = acc_ref[...].astype(BF16)


def _sb_attention(qb, kb, vb, cache, tq, tk):
    b, t, _ = qb.shape
    tile = pl.BlockSpec((1, tq, LANES), lambda bi, hi, qi: (bi, qi, hi))
    if cache is None:
        assert t % tq == 0 and tq % tk == 0
        past_blocks = lambda qi: qi * (tq // tk)
        full = pl.BlockSpec((None, t, LANES), lambda bi, hi, qi: (bi, 0, hi))
        k_past, v_past = kb, vb
    else:
        k_past, v_past, li = cache
        p = k_past.shape[3]
        assert t == tq and p % tk == 0
        past_blocks = lambda qi: jnp.int32(p // tk)
        full = pl.BlockSpec((None, None, LANES, p), lambda bi, hi, qi: (li, bi, hi, 0))
    return pl.pallas_call(
        functools.partial(_sb_kernel, tq=tq, tk=tk, past_blocks=past_blocks,
                          past_transposed=cache is not None),
        grid=(b, SB_DIM // LANES, t // tq),
        in_specs=[tile, tile, tile, full, full],
        out_specs=tile,
        out_shape=jax.ShapeDtypeStruct((b, t, SB_DIM), BF16),
        scratch_shapes=[
            pltpu.VMEM((tq, LANES), F32),
            pltpu.VMEM((tq, 1), F32),
            pltpu.VMEM((tq, 1), F32),
        ],
        compiler_params=_cparams(3),
        name="sb_attn",
    )(qb, kb, vb, k_past, v_past)


def _exp2_rows(s, m):
    n = s.shape[1] // LANES
    if n == 0:
        return jnp.exp2(s - m[:, :s.shape[1]]).astype(BF16)
    return jnp.concatenate(
        [jnp.exp2(s[:, c * LANES:(c + 1) * LANES] - m).astype(BF16) for c in range(n)], axis=1)


def _diff_kernel(*refs, tq, tk, row_chunk, head_stride, diag_in_past, past_blocks, q_pos0,
                 lam_init):
    if diag_in_past:
        q_ref, kp_ref, vp_ref = refs[:3]
        kd_ref = vd_ref = None
        rest = refs[3:]
    else:
        q_ref, kd_ref, vd_ref, kp_ref, vp_ref = refs[:5]
        rest = refs[5:]
    (lq1_ref, lk1_ref, lq2_ref, lk2_ref, g_ref, o_ref, m_ref, a_ref, p_ref, alpha_ref) = rest
    qi = pl.program_id(2)
    lane = lax.broadcasted_iota(jnp.int32, (tq, LANES), 1)
    first = lane < DIFF_HALF_DIM
    qf = q_ref[0].astype(F32)
    q2 = jnp.concatenate([jnp.where(first, qf, 0.0), jnp.where(first, 0.0, qf)],
                         axis=0).astype(BF16)
    n_max = kp_ref.shape[0] // (tk * head_stride)
    rows = min(2 * tq, row_chunk)

    def with_ones(v):
        return jnp.concatenate([v, jnp.ones(v.shape, BF16)], axis=1)

    def past(ref, j):
        start = jnp.clip(j, 0, n_max - 1) * tk
        if head_stride == 1:
            return ref[pl.ds(pl.multiple_of(start, tk), tk), :].astype(BF16)
        first_row = start * head_stride + pl.program_id(1)
        return ref[pl.ds(first_row, tk, stride=head_stride), :].astype(BF16)

    def own_mask(r, n_rows, n_keys):
        row = (r + lax.broadcasted_iota(jnp.int32, (n_rows, n_keys), 0)) % tq
        qpos = q_pos0(qi) + row
        kpos = q_pos0(qi) + lax.broadcasted_iota(jnp.int32, (n_rows, n_keys), 1)
        return (kpos // CHUNK) <= (qpos // CHUNK)

    def step(k, v1, own, pending_own=False):
        for r in range(0, 2 * tq, rows):
            sl = slice(r, r + rows)
            if v1 is not None:
                nk = r % tq + rows if pending_own else tk
                alpha = alpha_ref[sl]
                a_ref[sl] = (jnp.concatenate([alpha, alpha], axis=1) * a_ref[sl]
                             + _dot(p_ref[sl, :nk], v1[:nk]))
            if k is not None:
                nk = r % tq + rows if own else tk
                s = _dot_nt(q2[sl], k[:nk])
                if own:
                    s = jnp.where(own_mask(r, rows, nk), s, NEG_INF)
                m_old = m_ref[sl]
                m_new = jnp.maximum(m_old, jnp.max(s, axis=-1, keepdims=True))
                alpha_ref[sl] = jnp.exp2(m_old - m_new)
                m_ref[sl] = m_new
                p_ref[sl, :nk] = _exp2_rows(s, m_new)

    n_past = past_blocks(qi)
    first_step = (pl.program_id(0) == 0) & (pl.program_id(1) == 0) & (qi == 0)
    if diag_in_past:
        m_ref[...] = jnp.full(m_ref.shape, NEG_INF, F32)

        @pl.when(first_step)
        def _():
            a_ref[...] = jnp.zeros_like(a_ref)
            alpha_ref[...] = jnp.ones_like(alpha_ref)
    else:
        s = jnp.where(own_mask(0, 2 * tq, tq), _dot_nt(q2, kd_ref[0]), NEG_INF)
        m = jnp.broadcast_to(jnp.max(s, axis=-1, keepdims=True), (2 * tq, LANES))
        m_ref[...] = m
        a_ref[...] = _dot(_exp2_rows(s, m), with_ones(vd_ref[0]))
        alpha_ref[...] = jnp.ones_like(alpha_ref)

    @pl.when(first_step)
    def _():
        p_ref[...] = jnp.zeros_like(p_ref)

    def pending_values(j):
        v1 = with_ones(past(vp_ref, j))
        return jnp.where(j >= 0, v1, jnp.zeros_like(v1))

    def body(j, carry):
        step(past(kp_ref, j), pending_values(j - 1), False)
        return carry

    lax.fori_loop(0, n_past, body, 0)
    if diag_in_past:
        step(past(kp_ref, n_past), pending_values(n_past - 1), True)
        step(None, with_ones(past(vp_ref, n_past)), False, pending_own=True)
    else:
        step(None, with_ones(past(vp_ref, n_past - 1)), False)

    lam = (jnp.exp(jnp.sum(lq1_ref[...] * lk1_ref[...], axis=-1, keepdims=True))
           - jnp.exp(jnp.sum(lq2_ref[...] * lk2_ref[...], axis=-1, keepdims=True)) + lam_init)
    o = (a_ref[:tq, :LANES] / a_ref[:tq, LANES:]
         - lam * (a_ref[tq:, :LANES] / a_ref[tq:, LANES:]))
    o_ref[0] = (_rms(o, g_ref[...]) * (1.0 - lam_init)).astype(BF16)


def _diff_attention(qb, kb, vb, cache, lam_params, diff_g, tq, tk, lam_init):
    b, t, _ = qb.shape
    tile = pl.BlockSpec((1, tq, LANES), lambda bi, hi, qi: (bi, qi, hi))
    vec64 = pl.BlockSpec((1, DIFF_HALF_DIM), lambda bi, hi, qi: (0, 0))
    vec128 = pl.BlockSpec((1, DIFF_HEAD_DIM), lambda bi, hi, qi: (0, 0))
    if cache is None:
        assert tq == tk and t % tq == 0
        past_len, head_stride = 0, 1
        past_blocks = lambda qi: qi
        full = pl.BlockSpec((None, t, LANES), lambda bi, hi, qi: (bi, 0, hi))
        kv_specs, kv_args = [full, full], (kb, vb)
    else:
        k_cache, v_cache, li = cache
        rows = k_cache.shape[2]
        past_len, head_stride = rows // DIFF_HEADS, DIFF_HEADS
        assert t == tq and past_len % tk == 0
        past_blocks = lambda qi: past_len // tk
        full = pl.BlockSpec((None, None, rows, LANES), lambda bi, hi, qi: (li, bi, 0, 0))
        kv_specs, kv_args = [tile, tile, full, full], (kb, vb, k_cache, v_cache)
    q_pos0 = lambda qi: past_len + qi * tq
    return pl.pallas_call(
        functools.partial(_diff_kernel, tq=tq, tk=tk, row_chunk=128, head_stride=head_stride,
                          diag_in_past=cache is None, past_blocks=past_blocks, q_pos0=q_pos0,
                          lam_init=lam_init),
        grid=(b, DIFF_HEADS, t // tq),
        in_specs=[tile] + kv_specs + [vec64, vec64, vec64, vec64, vec128],
        out_specs=tile,
        out_shape=jax.ShapeDtypeStruct((b, t, DIFF_DIM), BF16),
        scratch_shapes=[
            pltpu.VMEM((2 * tq, LANES), F32), pltpu.VMEM((2 * tq, 2 * LANES), F32),
            pltpu.VMEM((2 * tq, tk), BF16), pltpu.VMEM((2 * tq, LANES), F32),
        ],
        compiler_params=pltpu.CompilerParams(
            dimension_semantics=("arbitrary",) * 3, vmem_limit_bytes=VMEM_LIMIT_BYTES),
        name="diff_attn",
    )(qb, *kv_args,
      *[a.reshape(1, DIFF_HALF_DIM) for a in lam_params], diff_g.reshape(1, DIFF_HEAD_DIM))


def _memkv_kernel(m_ref, g_ref, wk_ref, wv_ref, k_ref, v_ref):
    hb = _rms(m_ref[...], g_ref[...]).astype(BF16)
    k_ref[...] = _dot(hb, wk_ref[...])
    v_ref[...] = _dot(hb, wv_ref[...])


def _memkv(mem2d, g, wk_b, wv_b, li, tm):
    n = mem2d.shape[0]
    assert n % tm == 0
    row = pl.BlockSpec((tm, D_MODEL), lambda i: (i, 0))
    return pl.pallas_call(
        _memkv_kernel,
        grid=(n // tm,),
        in_specs=[row, pl.BlockSpec((1, D_MODEL), lambda i: (0, 0)),
                  _layer_weight(wk_b, li), _layer_weight(wv_b, li)],
        out_specs=[row, row],
        out_shape=[jax.ShapeDtypeStruct((n, D_MODEL), F32)] * 2,
        compiler_params=_cparams(1),
        name="memkv",
    )(mem2d, g.reshape(1, D_MODEL), wk_b, wv_b)


def _mix_mem_kernel(x_ref, pool_ref, sb_ref, df_ref, wout_ref, gpost0_ref, gpre1_ref, gpost1_ref,
                    wq_ref, mk_ref, mv_ref, wo_ref, o_ref, att_ref, mkb_ref, mvb_ref, *, sub):
    bb, tm, _ = x_ref.shape
    n_rows = bb * tm
    per = min(sub, tm)
    assert n_rows % sub == 0 and sub % per == 0 and (bb == 1 or sub == n_rows)

    @pl.when(pl.program_id(1) == 0)
    def _():
        for bi in range(bb):
            for h in range(MEM_HEADS):
                sl = slice(h * MEM_HEAD_DIM, (h + 1) * MEM_HEAD_DIM)
                mkb_ref[bi, :, sl] = mk_ref[bi, :, h, :].astype(BF16)
                mvb_ref[bi, :, sl] = mv_ref[bi, :, h, :].astype(BF16)

    def load(ref, rows):
        if bb == 1:
            return ref[0, rows, :]
        return ref[...].reshape(n_rows, ref.shape[-1])

    for r in range(0, n_rows, sub):
        rows = slice(r, r + sub)
        mixed = (_dot(load(pool_ref, rows), wout_ref[0:POOL_DIM, :])
                 + _dot(load(sb_ref, rows), wout_ref[POOL_DIM:POOL_DIM + SB_DIM, :])
                 + _dot(load(df_ref, rows), wout_ref[POOL_DIM + SB_DIM:, :]))
        x1 = load(x_ref, rows) + _rms(mixed, gpost0_ref[...])

        hb = _rms(x1, gpre1_ref[...]).astype(BF16)
        qm = (_dot(hb, wq_ref[...]) * (MEM_HEAD_DIM ** -0.5)).astype(BF16)
        for j in range(sub // per):
            bi = (r + j * per) // tm
            qrows = slice(j * per, (j + 1) * per)
            for h in range(MEM_HEADS):
                sl = slice(h * MEM_HEAD_DIM, (h + 1) * MEM_HEAD_DIM)
                s = _dot_nt(qm[qrows, sl], mkb_ref[bi, :, sl])
                e = jnp.exp(s - jnp.max(s, axis=-1, keepdims=True))
                p = e / jnp.sum(e, axis=-1, keepdims=True)
                att_ref[r + j * per:r + (j + 1) * per, sl] = _dot(
                    p.astype(BF16), mvb_ref[bi, :, sl]).astype(BF16)
        y = _dot(att_ref[rows, :], wo_ref[...])
        out = x1 + _rms(y, gpost1_ref[...])
        if bb == 1:
            o_ref[0, rows, :] = out
        else:
            o_ref[...] = out.reshape(bb, tm, out.shape[-1])


def _mix_mem(x, pool_o, sb_o, df_o, wout_b, g_post0, g_pre1, g_post1, wq_b, mem, wo_b, li, tm):
    b, t, _ = x.shape
    mk, mv = mem[0], mem[1]
    m = mk.shape[-3]
    if t % tm == 0:
        bb = 1
    else:
        tm, bb = t, max(d for d in range(1, b + 1) if b % d == 0 and d * t <= tm)
    tok = lambda w: pl.BlockSpec((bb, tm, w), lambda bi, ti: (bi, ti, 0))
    vec = pl.BlockSpec((1, D_MODEL), lambda bi, ti: (0, 0))
    if len(mem) == 2:
        memspec = pl.BlockSpec((bb, m, MEM_HEADS, MEM_HEAD_DIM), lambda bi, ti: (bi, 0, 0, 0),
                               pipeline_mode=pl.Buffered(1))
    else:
        mem_li = mem[2]
        memspec = pl.BlockSpec((None, bb, m, MEM_HEADS, MEM_HEAD_DIM),
                               lambda bi, ti: (mem_li, bi, 0, 0, 0),
                               pipeline_mode=pl.Buffered(1))
    return pl.pallas_call(
        functools.partial(_mix_mem_kernel, sub=min(bb * tm, 512)),
        grid=(b // bb, t // tm),
        in_specs=[tok(D_MODEL), tok(POOL_DIM), tok(SB_DIM), tok(DIFF_DIM),
                  _layer_weight(wout_b, li), vec, vec, vec,
                  _layer_weight(wq_b, li), memspec, memspec, _layer_weight(wo_b, li)],
        out_specs=tok(D_MODEL),
        out_shape=jax.ShapeDtypeStruct((b, t, D_MODEL), F32),
        scratch_shapes=[pltpu.VMEM((bb * tm, D_MODEL), BF16),
                        pltpu.VMEM((bb, m, D_MODEL), BF16), pltpu.VMEM((bb, m, D_MODEL), BF16)],
        compiler_params=pltpu.CompilerParams(
            dimension_semantics=("parallel", "arbitrary"), vmem_limit_bytes=VMEM_LIMIT_BYTES),
        name="mix_mem",
    )(x, pool_o, sb_o, df_o, wout_b, g_post0.reshape(1, -1), g_pre1.reshape(1, -1),
      g_post1.reshape(1, -1), wq_b, mk, mv, wo_b)


def _ffn_kernel(*refs, fc, sub, leaf_layers):
    x_ref, gpre_ref, gpost_ref, wg_ref, wu_ref, wd_ref = refs[:6]
    n_leaf = 4 * leaf_layers
    leaf_in, o_ref, leaf_out = refs[6:6 + n_leaf], refs[6 + n_leaf], refs[7 + n_leaf:]
    for layer in range(leaf_layers):
        for dst, src in zip(leaf_out, leaf_in[4 * layer:4 * layer + 4]):
            dst[layer] = src[0]
    for r in range(0, x_ref.shape[0], sub):
        x = x_ref[r:r + sub, :]
        hb = _rms(x, gpre_ref[...]).astype(BF16)
        acc = jnp.zeros(x.shape, F32)
        for c in range(D_FF // fc):
            sl = slice(c * fc, (c + 1) * fc)
            gate = _dot(hb, wg_ref[:, sl])
            up = _dot(hb, wu_ref[:, sl])
            act = (gate / (1.0 + jnp.exp(-gate))) * up
            acc = acc + _dot(act.astype(BF16), wd_ref[sl, :])
        o_ref[r:r + sub, :] = x + _rms(acc, gpost_ref[...])


def _ffn(x, g_pre2, g_post2, wg_b, wu_b, wd_b, li, tm, leaves=(), fc=D_FF):
    b, t, _ = x.shape
    n = b * t
    assert n % tm == 0 and D_FF % fc == 0
    row = pl.BlockSpec((tm, D_MODEL), lambda i: (i, 0))
    vec = pl.BlockSpec((1, D_MODEL), lambda i: (0, 0))
    leaf_in_specs = [_leaf_stack(kind, 1, b, t, tm)[1] for _ in leaves for kind in LEAF_KINDS]
    stacks = [_leaf_stack(kind, len(leaves), b, t, tm) for kind in LEAF_KINDS] if leaves else []
    outs = pl.pallas_call(
        functools.partial(_ffn_kernel, fc=fc, sub=min(tm, 256), leaf_layers=len(leaves)),
        grid=(n // tm,),
        in_specs=[row, vec, vec, _layer_weight(wg_b, li), _layer_weight(wu_b, li),
                  _layer_weight(wd_b, li)] + leaf_in_specs,
        out_specs=[row] + [spec for _, spec in stacks],
        out_shape=[jax.ShapeDtypeStruct((n, D_MODEL), F32)]
        + [jax.ShapeDtypeStruct(shape, F32) for shape, _ in stacks],
        compiler_params=_cparams(1),
        name="ffn",
    )(x.reshape(n, D_MODEL), g_pre2.reshape(1, -1), g_post2.reshape(1, -1), wg_b, wu_b, wd_b,
      *[a for layer in leaves for a in layer])
    return outs[0].reshape(b, t, D_MODEL), tuple(outs[1:])


def _pick(n, pref):
    return pref if n % pref == 0 else n


def _layer(x, pool_buf, sb_cache, diff_cache, mem, lw, li, lam_init, prev_leaves, last):
    (g_pre, g_post, w_in_b, w_out_b, wbd_b, pool_scale, lam_params, diff_g,
     wq_b, wo_b, wg_b, wu_b, wd_b) = lw
    b, t, _ = x.shape
    n = b * t
    past = 0 if sb_cache is None else sb_cache[0].shape[3]

    (u, sk, sv, dk, dv, sqb, skb, svb, dqb, dkb, dvb) = _inproj(
        x, g_pre[0], w_in_b, li, _pick(n, 512))
    r3 = lambda a: a.reshape(b, t, a.shape[-1])
    u, sqb, skb, svb, dqb, dkb, dvb = map(r3, (u, sqb, skb, svb, dqb, dkb, dvb))

    buf16 = jnp.concatenate([jnp.zeros((b, 1, POOL_DIM), F32), pool_buf], axis=1)
    pool_o = _pool(u, buf16, wbd_b, pool_scale, _pick(t, 512), past)
    new_pool = jnp.concatenate([pool_buf, u], axis=1)[:, -POOL_STATE:]

    if past:
        sb_o = _sb_attention(sqb, skb, svb, sb_cache, t, _pick(past, 256))
        df_o = _diff_attention(dqb, dkb, dvb, diff_cache, lam_params, diff_g,
                               t, _pick(past, 2048), lam_init)
    else:
        sb_tq = _pick(t, 256)
        sb_o = _sb_attention(sqb, skb, svb, None, sb_tq, sb_tq)
        df_tq = _pick(t, 1024)
        df_o = _diff_attention(dqb, dkb, dvb, None, lam_params, diff_g, df_tq, df_tq, lam_init)

    x = _mix_mem(x, pool_o, sb_o, df_o, w_out_b, g_post[0], g_pre[1], g_post[1],
                 wq_b, mem, wo_b, li, 512)
    leaves = prev_leaves + [(sk, sv, dk, dv)]
    x, stacks = _ffn(x, g_pre[2], g_post[2], wg_b, wu_b, wd_b, li, _pick(n, 512),
                     leaves if last else ())
    return x, (stacks if last else leaves), new_pool


def _finish_leaves(leaves, b, t):
    sk, sv, dk, dv = leaves
    nl = sk.shape[0]
    if sk.ndim == 4:
        sb = lambda a: a.reshape(nl, b, SB_HEADS, SB_HEAD_DIM, t).transpose(0, 1, 4, 2, 3)
    else:
        sb = lambda a: a.reshape(nl, b, t, SB_HEADS, SB_HEAD_DIM)
    df = lambda a: a.reshape(nl, b, t, DIFF_HEADS, DIFF_HEAD_DIM)
    return sb(sk), sb(sv), df(dk), df(dv)


def _block_diag(pool_w):
    g, c, d = pool_w.shape
    out = jnp.zeros((g * c, g * d), pool_w.dtype)
    for i in range(g):
        out = lax.dynamic_update_slice(out, pool_w[i], (i * c, i * d))
    return out


def kernel(x_prompt, x_sample, cache_sb_k, cache_sb_v, cache_diff_k, cache_diff_v, cache_mem_k, cache_mem_v, state_pool, mem_prompt, g_pre, g_post, g_mem, w_in, w_out, pool_w, pool_scale, lam_q1, lam_k1, lam_q2, lam_k2, diff_g, wq_m, wk_m, wv_m, wo_m, w_gate, w_up, w_down):
    depth = w_in.shape[0]
    xp, xs = x_prompt, x_sample
    bp, mem_len = mem_prompt.shape[0], mem_prompt.shape[1]
    dshape = cache_diff_k.shape
    diff_k_rows = cache_diff_k.reshape(dshape[0], dshape[1], dshape[2] * dshape[3], dshape[4])
    diff_v_rows = cache_diff_v.reshape(dshape[0], dshape[1], dshape[2] * dshape[3], dshape[4])
    sshape = cache_sb_k.shape
    to_feature_major = lambda c: c.transpose(0, 1, 3, 4, 2).reshape(
        sshape[0], sshape[1], sshape[3] * sshape[4], sshape[2])
    sb_kT, sb_vT = to_feature_major(cache_sb_k), to_feature_major(cache_sb_v)
    (w_in_b, w_out_b, wq_b, wk_b, wv_b, wo_b, wg_b, wu_b, wd_b) = [
        w.astype(BF16) for w in (w_in, w_out, wq_m, wk_m, wv_m, wo_m, w_gate, w_up, w_down)]
    p_pool, p_mk, p_mv, s_pool = [], [], [], []
    p_leaves, s_leaves = [], []
    for li in range(depth):
        lam_init = 0.8 - 0.6 * math.exp(-0.3 * li)
        lw = (g_pre[li], g_post[li], w_in_b, w_out_b, _block_diag(pool_w[li]).astype(BF16),
              pool_scale[li], (lam_q1[li], lam_k1[li], lam_q2[li], lam_k2[li]), diff_g[li],
              wq_b, wo_b, wg_b, wu_b, wd_b)
        mk2d, mv2d = _memkv(mem_prompt.reshape(bp * mem_len, D_MODEL), g_mem[li],
                            wk_b, wv_b, li, _pick(bp * mem_len, 256))
        mk = mk2d.reshape(bp, mem_len, MEM_HEADS, MEM_HEAD_DIM)
        mv = mv2d.reshape(bp, mem_len, MEM_HEADS, MEM_HEAD_DIM)
        zero_buf = jnp.zeros((xp.shape[0], POOL_STATE, POOL_DIM), xp.dtype)
        last = li == depth - 1
        xp, p_leaves, npool = _layer(xp, zero_buf, None, None, (mk, mv), lw, li, lam_init,
                                     p_leaves, last)
        p_pool.append(npool); p_mk.append(mk); p_mv.append(mv)
        xs, s_leaves, npool = _layer(xs, state_pool[li], (sb_kT, sb_vT, li),
                                     (diff_k_rows, diff_v_rows, li),
                                     (cache_mem_k, cache_mem_v, li), lw, li, lam_init, s_leaves,
                                     last)
        s_pool.append(npool)
    return (xp, xs,
            *_finish_leaves(p_leaves, xp.shape[0], xp.shape[1]),
            jnp.stack(p_pool), jnp.stack(p_mk), jnp.stack(p_mv),
            *_finish_leaves(s_leaves, xs.shape[0], xs.shape[1]),
            jnp.stack(s_pool))
```

```python
import functools
import math

import jax
import jax.numpy as jnp
from jax import lax
from jax.experimental import pallas as pl
from jax.experimental.pallas import tpu as pltpu

F32 = jnp.float32
BF16 = jnp.bfloat16

D_MODEL = 1024
CHUNK = 64
POOL_WINDOWS = (2, 4, 8, 16)
POOL_GROUP_DIM = 64
POOL_DIM = 256
POOL_STATE = 15
SB_HEADS = 4
SB_HEAD_DIM = 64
SB_DIM = 256
DIFF_HEADS = 4
DIFF_HALF_DIM = 64
DIFF_HEAD_DIM = 128
DIFF_DIM = 512
MEM_HEADS = 4
MEM_HEAD_DIM = 256
D_FF = 2816
EPS = 1e-6
NEG_INF = -1e30
LOG2E = 1.4426950408889634

LANES = 128
MXU_DIM = 256
VMEM_LIMIT_BYTES = 56 * 1024 * 1024

SB_ZERO_LOG = 110.0


def _cparams(n_grid):
    return pltpu.CompilerParams(
        dimension_semantics=("parallel",) * n_grid,
        vmem_limit_bytes=VMEM_LIMIT_BYTES,
    )


def _rms(x, g):
    return x * lax.rsqrt(jnp.mean(x * x, axis=-1, keepdims=True) + EPS) * g


def _dot(a, b):
    return jnp.dot(a, b, preferred_element_type=F32)


def _dot_nt(a, b):
    return lax.dot_general(a, b, (((1,), (1,)), ((), ())), preferred_element_type=F32)


def _pool_tile(u, xs_ref, pos0, wbd, scale):
    tm = u.shape[0]
    xs_ref[16:, :] = u
    pos = pos0 + lax.broadcasted_iota(jnp.int32, (tm, POOL_DIM), 0)
    group = lax.broadcasted_iota(jnp.int32, (tm, POOL_DIM), 1) // POOL_GROUP_DIM
    mean = jnp.zeros((tm, POOL_DIM), F32)
    assert POOL_WINDOWS == (2, 4, 8, 16)
    run = xs_ref[...]
    for gi, w in enumerate(POOL_WINDOWS):
        run = run + pltpu.roll(run, w // 2, 0)
        cnt = jnp.minimum(w, pos + 1).astype(F32)
        mean = jnp.where(group == gi, run[16:] / cnt, mean)
    d = (mean - u).astype(BF16)
    return (_dot(d, wbd) * scale).astype(BF16)


def _inproj_kernel(x_ref, g_ref, w_ref,
                   u_ref, sk_ref, sv_ref, dk_ref, dv_ref,
                   sqb_ref, skb_ref, svb_ref, dqb_ref, dkb_ref, dvb_ref, *, transpose_kv):
    hb = _rms(x_ref[...], g_ref[...]).astype(BF16)

    def cols(lo, hi):
        return _dot(hb, w_ref[:, lo:hi])

    u_ref[...] = cols(0, 256)
    sqb_ref[...] = (cols(256, 512) * (LOG2E * SB_HEAD_DIM ** -0.5)).astype(BF16)
    sk = cols(512, 768)
    skb_ref[...] = sk.astype(BF16)
    sv = cols(768, 1024)
    svb_ref[...] = sv.astype(BF16)
    if transpose_kv:
        sk_ref[0, 0] = sk.T
        sv_ref[0, 0] = sv.T
    else:
        sk_ref[0] = sk
        sv_ref[0] = sv
    dqb_ref[...] = (cols(1024, 1536) * (LOG2E * DIFF_HALF_DIM ** -0.5)).astype(BF16)
    tm = x_ref.shape[0]
    dk = cols(1536, 2048)
    dkb_ref[...] = dk.astype(BF16)
    dv = cols(2048, 2560)
    dvb_ref[...] = dv.astype(BF16)
    for h in range(DIFF_HEADS):
        sl = slice(h * DIFF_HEAD_DIM, (h + 1) * DIFF_HEAD_DIM)
        dk_ref[0, pl.ds(h, tm, stride=DIFF_HEADS), :] = dk[:, sl]
        dv_ref[0, pl.ds(h, tm, stride=DIFF_HEADS), :] = dv[:, sl]


def _layer_weight(w, li):
    return pl.BlockSpec((None,) + w.shape[1:], lambda *_: (li, 0, 0),
                        pipeline_mode=pl.Buffered(1))


LEAF_KINDS = ("sb", "sb", "diff", "diff")


def _leaf_stack(kind, layers, b, t, tm):
    n = b * t
    if kind == "diff":
        return ((layers, n * DIFF_HEADS, DIFF_HEAD_DIM),
                pl.BlockSpec((layers, tm * DIFF_HEADS, DIFF_HEAD_DIM), lambda i: (0, i, 0)))
    if t % tm == 0 and tm % LANES == 0:
        per_batch = t // tm
        return ((layers, b, SB_DIM, t),
                pl.BlockSpec((layers, 1, SB_DIM, tm),
                             lambda i: (0, i // per_batch, 0, i % per_batch)))
    return (layers, n, SB_DIM), pl.BlockSpec((layers, tm, SB_DIM), lambda i: (0, i, 0))


def _inproj(x, g, w_in_b, li, tm):
    b, t, _ = x.shape
    n = b * t
    assert n % tm == 0
    transpose_kv = t % tm == 0 and tm % LANES == 0
    widths = (256, 256, 256, 512, 512, 256, 256, 256, 512, 512, 512)
    dtypes = (F32,) * 5 + (BF16,) * 6
    out_shape = [jax.ShapeDtypeStruct((n, w), dt) for w, dt in zip(widths, dtypes)]
    out_specs = [pl.BlockSpec((tm, w), lambda i: (i, 0)) for w in widths]
    for i, kind in zip((1, 2, 3, 4), LEAF_KINDS):
        shape, out_specs[i] = _leaf_stack(kind, 1, b, t, tm)
        out_shape[i] = jax.ShapeDtypeStruct(shape, F32)
    return pl.pallas_call(
        functools.partial(_inproj_kernel, transpose_kv=transpose_kv),
        grid=(n // tm,),
        in_specs=[
            pl.BlockSpec((tm, D_MODEL), lambda i: (i, 0)),
            pl.BlockSpec((1, D_MODEL), lambda i: (0, 0)),
            _layer_weight(w_in_b, li),
        ],
        out_specs=out_specs,
        out_shape=out_shape,
        compiler_params=_cparams(1),
        name="inproj",
    )(x.reshape(n, D_MODEL), g.reshape(1, D_MODEL), w_in_b)


def _pool_kernel(u_ref, prev_ref, buf_ref, wbd_ref, scale_ref, o_ref, xs_ref, *, tm, start_pos):
    t = pl.program_id(1)
    xs_ref[0:16, :] = jnp.where(t == 0, buf_ref[0], prev_ref[0])
    o_ref[0] = _pool_tile(u_ref[0], xs_ref, start_pos + t * tm, wbd_ref[...], scale_ref[...])


def _pool(u, buf16, wbd_b, pool_scale, tm, start_pos):
    b, t, _ = u.shape
    assert t % tm == 0 and tm % 16 == 0
    r = tm // 16
    return pl.pallas_call(
        functools.partial(_pool_kernel, tm=tm, start_pos=start_pos),
        grid=(b, t // tm),
        in_specs=[
            pl.BlockSpec((1, tm, POOL_DIM), lambda bi, ti: (bi, ti, 0)),
            pl.BlockSpec((1, 16, POOL_DIM), lambda bi, ti: (bi, jnp.maximum(ti * r - 1, 0), 0)),
            pl.BlockSpec((1, 16, POOL_DIM), lambda bi, ti: (bi, 0, 0)),
            pl.BlockSpec((POOL_DIM, POOL_DIM), lambda bi, ti: (0, 0)),
            pl.BlockSpec((1, POOL_DIM), lambda bi, ti: (0, 0)),
        ],
        out_specs=pl.BlockSpec((1, tm, POOL_DIM), lambda bi, ti: (bi, ti, 0)),
        out_shape=jax.ShapeDtypeStruct((b, t, POOL_DIM), BF16),
        scratch_shapes=[pltpu.VMEM((tm + 16, POOL_DIM), F32)],
        compiler_params=_cparams(2),
        name="pool",
    )(u, u, buf16, wbd_b, pool_scale.reshape(1, POOL_DIM))


def _suffix_ones(n):
    r = lax.broadcasted_iota(jnp.int32, (n, n), 0)
    c = lax.broadcasted_iota(jnp.int32, (n, n), 1)
    return jnp.where(r >= c, 1.0, 0.0).astype(BF16)


def _sb_kernel(q_ref, kd_ref, vd_ref, kp_ref, vp_ref, o_ref, acc_ref, ca_ref, cb_ref,
               *, tq, tk, past_blocks, past_transposed):
    qi = pl.program_id(2)
    lane = lax.broadcasted_iota(jnp.int32, (tq, LANES), 1)
    first = lane < SB_HEAD_DIM
    qf = q_ref[0].astype(F32)
    qa = jnp.where(first, qf, 0.0).astype(BF16)
    qb = jnp.where(first, 0.0, qf).astype(BF16)

    def head(qh, k, mask, ones, transposed):
        z = _dot(qh, k) if transposed else _dot_nt(qh, k)
        sp = jnp.maximum(z, 0.0) + jnp.log2(1.0 + jnp.exp2(-jnp.abs(z)))
        if mask is not None:
            sp = jnp.where(mask, sp, 0.0)
        hi = sp.astype(BF16)
        lo = (sp - hi.astype(F32)).astype(BF16)
        return z, _dot(hi, ones) + _dot(lo, ones)

    def weigh(za, csa, zb, csb, ca, cb, v, mask, transposed):
        wa = jnp.exp2(za - csa - ca)
        wb = jnp.exp2(zb - csb - cb)
        if mask is not None:
            wa = jnp.where(mask, wa, 0.0)
            wb = jnp.where(mask, wb, 0.0)
        apply = _dot_nt if transposed else _dot
        pv = jnp.where(first, apply(wa.astype(BF16), v), apply(wb.astype(BF16), v))
        return pv, ca + csa[:, 0:1], cb + csb[:, 0:1]

    def past(j):
        start = pl.multiple_of(jnp.maximum(j, 0) * tk, tk)
        if past_transposed:
            return (kp_ref[:, pl.ds(start, tk)].astype(BF16),
                    vp_ref[:, pl.ds(start, tk)].astype(BF16))
        return kp_ref[pl.ds(start, tk), :].astype(BF16), vp_ref[pl.ds(start, tk), :].astype(BF16)

    n_past = past_blocks(qi)
    ones_p = _suffix_ones(tk)
    ones_o = _suffix_ones(tq)
    row = lax.broadcasted_iota(jnp.int32, (tq, tq), 0)
    col = lax.broadcasted_iota(jnp.int32, (tq, tq), 1)
    own = col < row
    have_prev = n_past > 0
    kd, vd = kd_ref[0], vd_ref[0]
    k1, v1 = past(n_past - 1)
    za, csa = head(qa, kd, own, ones_o, False)
    zb, csb = head(qb, kd, own, ones_o, False)
    za1, csa1 = head(qa, k1, have_prev, ones_p, past_transposed)
    zb1, csb1 = head(qb, k1, have_prev, ones_p, past_transposed)
    pv0, ca, cb = weigh(za, csa, zb, csb, 0.0, 0.0, vd, own, False)
    pv1, ca, cb = weigh(za1, csa1, zb1, csb1, ca, cb, v1, have_prev, past_transposed)
    acc_ref[...] = pv0 + pv1
    ca_ref[...] = ca
    cb_ref[...] = cb

    def cond(state):
        j, cm = state
        return jnp.logical_and(j >= 0, cm <= SB_ZERO_LOG * LOG2E)

    def body(state):
        j, _ = state
        k, v = past(j)
        za, csa = head(qa, k, None, ones_p, past_transposed)
        zb, csb = head(qb, k, None, ones_p, past_transposed)
        pv, ca, cb = weigh(za, csa, zb, csb, ca_ref[...], cb_ref[...], v, None, past_transposed)
        acc_ref[...] += pv
        ca_ref[...] = ca
        cb_ref[...] = cb
        return j - 1, jnp.min(jnp.minimum(ca, cb))

    lax.while_loop(cond, body, (n_past - 2, jnp.min(jnp.minimum(ca, cb))))
    o_ref[0] = acc_ref[...].astype(BF16)


def _sb_attention(qb, kb, vb, cache, tq, tk):
    b, t, _ = qb.shape
    tile = pl.BlockSpec((1, tq, LANES), lambda bi, hi, qi: (bi, qi, hi))
    if cache is None:
        assert t % tq == 0 and tq % tk == 0
        past_blocks = lambda qi: qi * (tq // tk)
        full = pl.BlockSpec((None, t, LANES), lambda bi, hi, qi: (bi, 0, hi))
        k_past, v_past = kb, vb
    else:
        k_past, v_past, li = cache
        p = k_past.shape[3]
        assert t == tq and p % tk == 0
        past_blocks = lambda qi: jnp.int32(p // tk)
        full = pl.BlockSpec((None, None, LANES, p), lambda bi, hi, qi: (li, bi, hi, 0))
    return pl.pallas_call(
        functools.partial(_sb_kernel, tq=tq, tk=tk, past_blocks=past_blocks,
                          past_transposed=cache is not None),
        grid=(b, SB_DIM // LANES, t // tq),
        in_specs=[tile, tile, tile, full, full],
        out_specs=tile,
        out_shape=jax.ShapeDtypeStruct((b, t, SB_DIM), BF16),
        scratch_shapes=[
            pltpu.VMEM((tq, LANES), F32),
            pltpu.VMEM((tq, 1), F32),
            pltpu.VMEM((tq, 1), F32),
        ],
        compiler_params=_cparams(3),
        name="sb_attn",
    )(qb, kb, vb, k_past, v_past)


def _exp2_rows(s, m):
    n = s.shape[1] // LANES
    if n == 0:
        return jnp.exp2(s - m[:, :s.shape[1]]).astype(BF16)
    return jnp.concatenate(
        [jnp.exp2(s[:, c * LANES:(c + 1) * LANES] - m).astype(BF16) for c in range(n)], axis=1)


def _diff_kernel(*refs, tq, tk, row_chunk, head_stride, diag_in_past, past_blocks, q_pos0,
                 lam_init):
    if diag_in_past:
        q_ref, kp_ref, vp_ref = refs[:3]
        kd_ref = vd_ref = None
        rest = refs[3:]
    else:
        q_ref, kd_ref, vd_ref, kp_ref, vp_ref = refs[:5]
        rest = refs[5:]
    (lq1_ref, lk1_ref, lq2_ref, lk2_ref, g_ref, o_ref, m_ref, a_ref, p_ref, alpha_ref) = rest
    qi = pl.program_id(2)
    lane = lax.broadcasted_iota(jnp.int32, (tq, LANES), 1)
    first = lane < DIFF_HALF_DIM
    qf = q_ref[0].astype(F32)
    q2 = jnp.concatenate([jnp.where(first, qf, 0.0), jnp.where(first, 0.0, qf)],
                         axis=0).astype(BF16)
    n_max = kp_ref.shape[0] // (tk * head_stride)
    rows = min(2 * tq, row_chunk)

    def with_ones(v):
        return jnp.concatenate([v, jnp.ones(v.shape, BF16)], axis=1)

    def past(ref, j):
        start = jnp.clip(j, 0, n_max - 1) * tk
        if head_stride == 1:
            return ref[pl.ds(pl.multiple_of(start, tk), tk), :].astype(BF16)
        first_row = start * head_stride + pl.program_id(1)
        return ref[pl.ds(first_row, tk, stride=head_stride), :].astype(BF16)

    def own_mask(r, n_rows, n_keys):
        row = (r + lax.broadcasted_iota(jnp.int32, (n_rows, n_keys), 0)) % tq
        qpos = q_pos0(qi) + row
        kpos = q_pos0(qi) + lax.broadcasted_iota(jnp.int32, (n_rows, n_keys), 1)
        return (kpos // CHUNK) <= (qpos // CHUNK)

    def step(k, v1, own, pending_own=False):
        for r in range(0, 2 * tq, rows):
            sl = slice(r, r + rows)
            if v1 is not None:
                nk = r % tq + rows if pending_own else tk
                alpha = alpha_ref[sl]
                a_ref[sl] = (jnp.concatenate([alpha, alpha], axis=1) * a_ref[sl]
                             + _dot(p_ref[sl, :nk], v1[:nk]))
            if k is not None:
                nk = r % tq + rows if own else tk
                s = _dot_nt(q2[sl], k[:nk])
                if own:
                    s = jnp.where(own_mask(r, rows, nk), s, NEG_INF)
                m_old = m_ref[sl]
                m_new = jnp.maximum(m_old, jnp.max(s, axis=-1, keepdims=True))
                alpha_ref[sl] = jnp.exp2(m_old - m_new)
                m_ref[sl] = m_new
                p_ref[sl, :nk] = _exp2_rows(s, m_new)

    n_past = past_blocks(qi)
    first_step = (pl.program_id(0) == 0) & (pl.program_id(1) == 0) & (qi == 0)
    if diag_in_past:
        m_ref[...] = jnp.full(m_ref.shape, NEG_INF, F32)

        @pl.when(first_step)
        def _():
            a_ref[...] = jnp.zeros_like(a_ref)
            alpha_ref[...] = jnp.ones_like(alpha_ref)
    else:
        s = jnp.where(own_mask(0, 2 * tq, tq), _dot_nt(q2, kd_ref[0]), NEG_INF)
        m = jnp.broadcast_to(jnp.max(s, axis=-1, keepdims=True), (2 * tq, LANES))
        m_ref[...] = m
        a_ref[...] = _dot(_exp2_rows(s, m), with_ones(vd_ref[0]))
        alpha_ref[...] = jnp.ones_like(alpha_ref)

    @pl.when(first_step)
    def _():
        p_ref[...] = jnp.zeros_like(p_ref)

    def pending_values(j):
        v1 = with_ones(past(vp_ref, j))
        return jnp.where(j >= 0, v1, jnp.zeros_like(v1))

    def body(j, carry):
        step(past(kp_ref, j), pending_values(j - 1), False)
        return carry

    lax.fori_loop(0, n_past, body, 0)
    if diag_in_past:
        step(past(kp_ref, n_past), pending_values(n_past - 1), True)
        step(None, with_ones(past(vp_ref, n_past)), False, pending_own=True)
    else:
        step(None, with_ones(past(vp_ref, n_past - 1)), False)

    lam = (jnp.exp(jnp.sum(lq1_ref[...] * lk1_ref[...], axis=-1, keepdims=True))
           - jnp.exp(jnp.sum(lq2_ref[...] * lk2_ref[...], axis=-1, keepdims=True)) + lam_init)
    o = (a_ref[:tq, :LANES] / a_ref[:tq, LANES:]
         - lam * (a_ref[tq:, :LANES] / a_ref[tq:, LANES:]))
    o_ref[0] = (_rms(o, g_ref[...]) * (1.0 - lam_init)).astype(BF16)


def _diff_attention(qb, kb, vb, cache, lam_params, diff_g, tq, tk, lam_init):
    b, t, _ = qb.shape
    tile = pl.BlockSpec((1, tq, LANES), lambda bi, hi, qi: (bi, qi, hi))
    vec64 = pl.BlockSpec((1, DIFF_HALF_DIM), lambda bi, hi, qi: (0, 0))
    vec128 = pl.BlockSpec((1, DIFF_HEAD_DIM), lambda bi, hi, qi: (0, 0))
    if cache is None:
        assert tq == tk and t % tq == 0
        past_len, head_stride = 0, 1
        past_blocks = lambda qi: qi
        full = pl.BlockSpec((None, t, LANES), lambda bi, hi, qi: (bi, 0, hi))
        kv_specs, kv_args = [full, full], (kb, vb)
    else:
        k_cache, v_cache, li = cache
        rows = k_cache.shape[2]
        past_len, head_stride = rows // DIFF_HEADS, DIFF_HEADS
        assert t == tq and past_len % tk == 0
        past_blocks = lambda qi: past_len // tk
        full = pl.BlockSpec((None, None, rows, LANES), lambda bi, hi, qi: (li, bi, 0, 0))
        kv_specs, kv_args = [tile, tile, full, full], (kb, vb, k_cache, v_cache)
    q_pos0 = lambda qi: past_len + qi * tq
    return pl.pallas_call(
        functools.partial(_diff_kernel, tq=tq, tk=tk, row_chunk=128, head_stride=head_stride,
                          diag_in_past=cache is None, past_blocks=past_blocks, q_pos0=q_pos0,
                          lam_init=lam_init),
        grid=(b, DIFF_HEADS, t // tq),
        in_specs=[tile] + kv_specs + [vec64, vec64, vec64, vec64, vec128],
        out_specs=tile,
        out_shape=jax.ShapeDtypeStruct((b, t, DIFF_DIM), BF16),
        scratch_shapes=[
            pltpu.VMEM((2 * tq, LANES), F32), pltpu.VMEM((2 * tq, 2 * LANES), F32),
            pltpu.VMEM((2 * tq, tk), BF16), pltpu.VMEM((2 * tq, LANES), F32),
        ],
        compiler_params=pltpu.CompilerParams(
            dimension_semantics=("arbitrary",) * 3, vmem_limit_bytes=VMEM_LIMIT_BYTES),
        name="diff_attn",
    )(qb, *kv_args,
      *[a.reshape(1, DIFF_HALF_DIM) for a in lam_params], diff_g.reshape(1, DIFF_HEAD_DIM))


def _memkv_kernel(m_ref, g_ref, wk_ref, wv_ref, k_ref, v_ref):
    hb = _rms(m_ref[...], g_ref[...]).astype(BF16)
    k_ref[...] = _dot(hb, wk_ref[...])
    v_ref[...] = _dot(hb, wv_ref[...])


def _memkv(mem2d, g, wk_b, wv_b, li, tm):
    n = mem2d.shape[0]
    assert n % tm == 0
    row = pl.BlockSpec((tm, D_MODEL), lambda i: (i, 0))
    return pl.pallas_call(
        _memkv_kernel,
        grid=(n // tm,),
        in_specs=[row, pl.BlockSpec((1, D_MODEL), lambda i: (0, 0)),
                  _layer_weight(wk_b, li), _layer_weight(wv_b, li)],
        out_specs=[row, row],
        out_shape=[jax.ShapeDtypeStruct((n, D_MODEL), F32)] * 2,
        compiler_params=_cparams(1),
        name="memkv",
    )(mem2d, g.reshape(1, D_MODEL), wk_b, wv_b)


def _mix_mem_kernel(x_ref, pool_ref, sb_ref, df_ref, wout_ref, gpost0_ref, gpre1_ref, gpost1_ref,
                    wq_ref, mk_ref, mv_ref, wo_ref, o_ref, att_ref, mkb_ref, mvb_ref, *, sub):
    @pl.when(pl.program_id(1) == 0)
    def _():
        for h in range(MEM_HEADS):
            sl = slice(h * MEM_HEAD_DIM, (h + 1) * MEM_HEAD_DIM)
            mkb_ref[:, sl] = mk_ref[:, h, :].astype(BF16)
            mvb_ref[:, sl] = mv_ref[:, h, :].astype(BF16)

    tiles = [slice(r, r + sub) for r in range(0, x_ref.shape[1], sub)]
    mixed = [_dot(pool_ref[0, rows, :], wout_ref[0:POOL_DIM, :])
             + _dot(sb_ref[0, rows, :], wout_ref[POOL_DIM:POOL_DIM + SB_DIM, :])
             + _dot(df_ref[0, rows, :], wout_ref[POOL_DIM + SB_DIM:, :]) for rows in tiles]
    x1 = [x_ref[0, rows, :] + _rms(mx, gpost0_ref[...]) for rows, mx in zip(tiles, mixed)]
    hb = [_rms(xi, gpre1_ref[...]).astype(BF16) for xi in x1]
    qm = [(_dot(hi, wq_ref[...]) * (MEM_HEAD_DIM ** -0.5)).astype(BF16) for hi in hb]
    for rows, qi in zip(tiles, qm):
        for h in range(MEM_HEADS):
            sl = slice(h * MEM_HEAD_DIM, (h + 1) * MEM_HEAD_DIM)
            s = _dot_nt(qi[:, sl], mkb_ref[:, sl])
            e = jnp.exp(s - jnp.max(s, axis=-1, keepdims=True))
            p = e / jnp.sum(e, axis=-1, keepdims=True)
            att_ref[rows, sl] = _dot(p.astype(BF16), mvb_ref[:, sl]).astype(BF16)
    y = [_dot(att_ref[rows, :], wo_ref[...]) for rows in tiles]
    for rows, xi, yi in zip(tiles, x1, y):
        o_ref[0, rows, :] = xi + _rms(yi, gpost1_ref[...])


def _mix_mem(x, pool_o, sb_o, df_o, wout_b, g_post0, g_pre1, g_post1, wq_b, mem, wo_b, li, tm):
    b, t, _ = x.shape
    assert t % tm == 0
    mk, mv = mem[0], mem[1]
    m = mk.shape[-3]
    tok = lambda w: pl.BlockSpec((1, tm, w), lambda bi, ti: (bi, ti, 0))
    vec = pl.BlockSpec((1, D_MODEL), lambda bi, ti: (0, 0))
    if len(mem) == 2:
        memspec = pl.BlockSpec((None, m, MEM_HEADS, MEM_HEAD_DIM), lambda bi, ti: (bi, 0, 0, 0))
    else:
        mem_li = mem[2]
        memspec = pl.BlockSpec((None, None, m, MEM_HEADS, MEM_HEAD_DIM),
                               lambda bi, ti: (mem_li, bi, 0, 0, 0))
    return pl.pallas_call(
        functools.partial(_mix_mem_kernel, sub=min(tm, 256)),
        grid=(b, t // tm),
        in_specs=[tok(D_MODEL), tok(POOL_DIM), tok(SB_DIM), tok(DIFF_DIM),
                  _layer_weight(wout_b, li), vec, vec, vec,
                  _layer_weight(wq_b, li), memspec, memspec, _layer_weight(wo_b, li)],
        out_specs=tok(D_MODEL),
        out_shape=jax.ShapeDtypeStruct((b, t, D_MODEL), F32),
        scratch_shapes=[pltpu.VMEM((tm, D_MODEL), BF16),
                        pltpu.VMEM((m, D_MODEL), BF16), pltpu.VMEM((m, D_MODEL), BF16)],
        compiler_params=pltpu.CompilerParams(
            dimension_semantics=("parallel", "arbitrary"), vmem_limit_bytes=VMEM_LIMIT_BYTES),
        name="mix_mem",
    )(x, pool_o, sb_o, df_o, wout_b, g_post0.reshape(1, -1), g_pre1.reshape(1, -1),
      g_post1.reshape(1, -1), wq_b, mk, mv, wo_b)


def _ffn_kernel(*refs, fc, sub, leaf_layers):
    x_ref, gpre_ref, gpost_ref, wg_ref, wu_ref, wd_ref = refs[:6]
    n_leaf = 4 * leaf_layers
    leaf_in, o_ref, leaf_out = refs[6:6 + n_leaf], refs[6 + n_leaf], refs[7 + n_leaf:]
    for layer in range(leaf_layers):
        for dst, src in zip(leaf_out, leaf_in[4 * layer:4 * layer + 4]):
            dst[layer] = src[0]
    for r in range(0, x_ref.shape[0], sub):
        x = x_ref[r:r + sub, :]
        hb = _rms(x, gpre_ref[...]).astype(BF16)
        acc = jnp.zeros(x.shape, F32)
        for c in range(D_FF // fc):
            sl = slice(c * fc, (c + 1) * fc)
            gate = _dot(hb, wg_ref[:, sl])
            up = _dot(hb, wu_ref[:, sl])
            act = (gate / (1.0 + jnp.exp(-gate))) * up
            acc = acc + _dot(act.astype(BF16), wd_ref[sl, :])
        o_ref[r:r + sub, :] = x + _rms(acc, gpost_ref[...])


def _ffn(x, g_pre2, g_post2, wg_b, wu_b, wd_b, li, tm, leaves=(), fc=D_FF):
    b, t, _ = x.shape
    n = b * t
    assert n % tm == 0 and D_FF % fc == 0
    row = pl.BlockSpec((tm, D_MODEL), lambda i: (i, 0))
    vec = pl.BlockSpec((1, D_MODEL), lambda i: (0, 0))
    leaf_in_specs = [_leaf_stack(kind, 1, b, t, tm)[1] for _ in leaves for kind in LEAF_KINDS]
    stacks = [_leaf_stack(kind, len(leaves), b, t, tm) for kind in LEAF_KINDS] if leaves else []
    outs = pl.pallas_call(
        functools.partial(_ffn_kernel, fc=fc, sub=min(tm, 256), leaf_layers=len(leaves)),
        grid=(n // tm,),
        in_specs=[row, vec, vec, _layer_weight(wg_b, li), _layer_weight(wu_b, li),
                  _layer_weight(wd_b, li)] + leaf_in_specs,
        out_specs=[row] + [spec for _, spec in stacks],
        out_shape=[jax.ShapeDtypeStruct((n, D_MODEL), F32)]
        + [jax.ShapeDtypeStruct(shape, F32) for shape, _ in stacks],
        compiler_params=_cparams(1),
        name="ffn",
    )(x.reshape(n, D_MODEL), g_pre2.reshape(1, -1), g_post2.reshape(1, -1), wg_b, wu_b, wd_b,
      *[a for layer in leaves for a in layer])
    return outs[0].reshape(b, t, D_MODEL), tuple(outs[1:])


def _pick(n, pref):
    return pref if n % pref == 0 else n


def _layer(x, pool_buf, sb_cache, diff_cache, mem, lw, li, lam_init, prev_leaves, last):
    (g_pre, g_post, w_in_b, w_out_b, wbd_b, pool_scale, lam_params, diff_g,
     wq_b, wo_b, wg_b, wu_b, wd_b) = lw
    b, t, _ = x.shape
    n = b * t
    past = 0 if sb_cache is None else sb_cache[0].shape[3]

    (u, sk, sv, dk, dv, sqb, skb, svb, dqb, dkb, dvb) = _inproj(
        x, g_pre[0], w_in_b, li, _pick(n, 512))
    r3 = lambda a: a.reshape(b, t, a.shape[-1])
    u, sqb, skb, svb, dqb, dkb, dvb = map(r3, (u, sqb, skb, svb, dqb, dkb, dvb))

    buf16 = jnp.concatenate([jnp.zeros((b, 1, POOL_DIM), F32), pool_buf], axis=1)
    pool_o = _pool(u, buf16, wbd_b, pool_scale, _pick(t, 512), past)
    new_pool = jnp.concatenate([pool_buf, u], axis=1)[:, -POOL_STATE:]

    if past:
        sb_o = _sb_attention(sqb, skb, svb, sb_cache, t, _pick(past, 256))
        df_o = _diff_attention(dqb, dkb, dvb, diff_cache, lam_params, diff_g,
                               t, _pick(past, 2048), lam_init)
    else:
        sb_tq = _pick(t, 256)
        sb_o = _sb_attention(sqb, skb, svb, None, sb_tq, sb_tq)
        df_tq = _pick(t, 1024)
        df_o = _diff_attention(dqb, dkb, dvb, None, lam_params, diff_g, df_tq, df_tq, lam_init)

    x = _mix_mem(x, pool_o, sb_o, df_o, w_out_b, g_post[0], g_pre[1], g_post[1],
                 wq_b, mem, wo_b, li, _pick(t, 1024))
    leaves = prev_leaves + [(sk, sv, dk, dv)]
    x, stacks = _ffn(x, g_pre[2], g_post[2], wg_b, wu_b, wd_b, li, _pick(n, 512),
                     leaves if last else ())
    return x, (stacks if last else leaves), new_pool


def _finish_leaves(leaves, b, t):
    sk, sv, dk, dv = leaves
    nl = sk.shape[0]
    if sk.ndim == 4:
        sb = lambda a: a.reshape(nl, b, SB_HEADS, SB_HEAD_DIM, t).transpose(0, 1, 4, 2, 3)
    else:
        sb = lambda a: a.reshape(nl, b, t, SB_HEADS, SB_HEAD_DIM)
    df = lambda a: a.reshape(nl, b, t, DIFF_HEADS, DIFF_HEAD_DIM)
    return sb(sk), sb(sv), df(dk), df(dv)


def _block_diag(pool_w):
    g, c, d = pool_w.shape
    out = jnp.zeros((g * c, g * d), pool_w.dtype)
    for i in range(g):
        out = lax.dynamic_update_slice(out, pool_w[i], (i * c, i * d))
    return out


def kernel(x_prompt, x_sample, cache_sb_k, cache_sb_v, cache_diff_k, cache_diff_v, cache_mem_k, cache_mem_v, state_pool, mem_prompt, g_pre, g_post, g_mem, w_in, w_out, pool_w, pool_scale, lam_q1, lam_k1, lam_q2, lam_k2, diff_g, wq_m, wk_m, wv_m, wo_m, w_gate, w_up, w_down):
    depth = w_in.shape[0]
    xp, xs = x_prompt, x_sample
    bp, mem_len = mem_prompt.shape[0], mem_prompt.shape[1]
    dshape = cache_diff_k.shape
    diff_k_rows = cache_diff_k.reshape(dshape[0], dshape[1], dshape[2] * dshape[3], dshape[4])
    diff_v_rows = cache_diff_v.reshape(dshape[0], dshape[1], dshape[2] * dshape[3], dshape[4])
    sshape = cache_sb_k.shape
    to_feature_major = lambda c: c.transpose(0, 1, 3, 4, 2).reshape(
        sshape[0], sshape[1], sshape[3] * sshape[4], sshape[2])
    sb_kT, sb_vT = to_feature_major(cache_sb_k), to_feature_major(cache_sb_v)
    (w_in_b, w_out_b, wq_b, wk_b, wv_b, wo_b, wg_b, wu_b, wd_b) = [
        w.astype(BF16) for w in (w_in, w_out, wq_m, wk_m, wv_m, wo_m, w_gate, w_up, w_down)]
    p_pool, p_mk, p_mv, s_pool = [], [], [], []
    p_leaves, s_leaves = [], []
    for li in range(depth):
        lam_init = 0.8 - 0.6 * math.exp(-0.3 * li)
        lw = (g_pre[li], g_post[li], w_in_b, w_out_b, _block_diag(pool_w[li]).astype(BF16),
              pool_scale[li], (lam_q1[li], lam_k1[li], lam_q2[li], lam_k2[li]), diff_g[li],
              wq_b, wo_b, wg_b, wu_b, wd_b)
        mk2d, mv2d = _memkv(mem_prompt.reshape(bp * mem_len, D_MODEL), g_mem[li],
                            wk_b, wv_b, li, _pick(bp * mem_len, 256))
        mk = mk2d.reshape(bp, mem_len, MEM_HEADS, MEM_HEAD_DIM)
        mv = mv2d.reshape(bp, mem_len, MEM_HEADS, MEM_HEAD_DIM)
        zero_buf = jnp.zeros((xp.shape[0], POOL_STATE, POOL_DIM), xp.dtype)
        last = li == depth - 1
        xp, p_leaves, npool = _layer(xp, zero_buf, None, None, (mk, mv), lw, li, lam_init,
                                     p_leaves, last)
        p_pool.append(npool); p_mk.append(mk); p_mv.append(mv)
        xs, s_leaves, npool = _layer(xs, state_pool[li], (sb_kT, sb_vT, li),
                                     (diff_k_rows, diff_v_rows, li),
                                     (cache_mem_k, cache_mem_v, li), lw, li, lam_init, s_leaves,
                                     last)
        s_pool.append(npool)
    return (xp, xs,
            *_finish_leaves(p_leaves, xp.shape[0], xp.shape[1]),
            jnp.stack(p_pool), jnp.stack(p_mk), jnp.stack(p_mv),
            *_finish_leaves(s_leaves, xs.shape[0], xs.shape[1]),
            jnp.stack(s_pool))
```

```python
import functools
import math

import jax
import jax.numpy as jnp
from jax import lax
from jax.experimental import pallas as pl
from jax.experimental.pallas import tpu as pltpu

F32 = jnp.float32
BF16 = jnp.bfloat16

D_MODEL = 1024
CHUNK = 64
POOL_WINDOWS = (2, 4, 8, 16)
POOL_GROUP_DIM = 64
POOL_DIM = 256
POOL_STATE = 15
SB_HEADS = 4
SB_HEAD_DIM = 64
SB_DIM = 256
DIFF_HEADS = 4
DIFF_HALF_DIM = 64
DIFF_HEAD_DIM = 128
DIFF_DIM = 512
MEM_HEADS = 4
MEM_HEAD_DIM = 256
D_FF = 2816
EPS = 1e-6
NEG_INF = -1e30
LOG2E = 1.4426950408889634

LANES = 128
MXU_DIM = 256
VMEM_LIMIT_BYTES = 56 * 1024 * 1024

SB_ZERO_LOG = 110.0


def _cparams(n_grid):
    return pltpu.CompilerParams(
        dimension_semantics=("parallel",) * n_grid,
        vmem_limit_bytes=VMEM_LIMIT_BYTES,
    )


def _rms(x, g):
    return x * lax.rsqrt(jnp.mean(x * x, axis=-1, keepdims=True) + EPS) * g


def _dot(a, b):
    return jnp.dot(a, b, preferred_element_type=F32)


def _dot_nt(a, b):
    return lax.dot_general(a, b, (((1,), (1,)), ((), ())), preferred_element_type=F32)


def _pool_tile(u, xs_ref, pos0, wbd, scale):
    tm = u.shape[0]
    xs_ref[16:, :] = u
    pos = pos0 + lax.broadcasted_iota(jnp.int32, (tm, POOL_DIM), 0)
    group = lax.broadcasted_iota(jnp.int32, (tm, POOL_DIM), 1) // POOL_GROUP_DIM
    mean = jnp.zeros((tm, POOL_DIM), F32)
    assert POOL_WINDOWS == (2, 4, 8, 16)
    run = xs_ref[...]
    for gi, w in enumerate(POOL_WINDOWS):
        run = run + pltpu.roll(run, w // 2, 0)
        cnt = jnp.minimum(w, pos + 1).astype(F32)
        mean = jnp.where(group == gi, run[16:] / cnt, mean)
    d = (mean - u).astype(BF16)
    return (_dot(d, wbd) * scale).astype(BF16)


def _inproj_kernel(x_ref, g_ref, w_ref,
                   u_ref, sk_ref, sv_ref, dk_ref, dv_ref,
                   sqb_ref, skb_ref, svb_ref, dqb_ref, dkb_ref, dvb_ref, *, transpose_kv):
    hb = _rms(x_ref[...], g_ref[...]).astype(BF16)

    def cols(lo, hi):
        return _dot(hb, w_ref[:, lo:hi])

    u_ref[...] = cols(0, 256)
    sqb_ref[...] = (cols(256, 512) * (LOG2E * SB_HEAD_DIM ** -0.5)).astype(BF16)
    sk = cols(512, 768)
    skb_ref[...] = sk.astype(BF16)
    sv = cols(768, 1024)
    svb_ref[...] = sv.astype(BF16)
    if transpose_kv:
        sk_ref[0, 0] = sk.T
        sv_ref[0, 0] = sv.T
    else:
        sk_ref[0] = sk
        sv_ref[0] = sv
    dqb_ref[...] = (cols(1024, 1536) * (LOG2E * DIFF_HALF_DIM ** -0.5)).astype(BF16)
    tm = x_ref.shape[0]
    dk = cols(1536, 2048)
    dkb_ref[...] = dk.astype(BF16)
    dv = cols(2048, 2560)
    dvb_ref[...] = dv.astype(BF16)
    for h in range(DIFF_HEADS):
        sl = slice(h * DIFF_HEAD_DIM, (h + 1) * DIFF_HEAD_DIM)
        dk_ref[0, pl.ds(h, tm, stride=DIFF_HEADS), :] = dk[:, sl]
        dv_ref[0, pl.ds(h, tm, stride=DIFF_HEADS), :] = dv[:, sl]


def _layer_weight(w, li):
    return pl.BlockSpec((None,) + w.shape[1:], lambda *_: (li, 0, 0),
                        pipeline_mode=pl.Buffered(1))


LEAF_KINDS = ("sb", "sb", "diff", "diff")


def _leaf_stack(kind, layers, b, t, tm):
    n = b * t
    if kind == "diff":
        return ((layers, n * DIFF_HEADS, DIFF_HEAD_DIM),
                pl.BlockSpec((layers, tm * DIFF_HEADS, DIFF_HEAD_DIM), lambda i: (0, i, 0)))
    if t % tm == 0 and tm % LANES == 0:
        per_batch = t // tm
        return ((layers, b, SB_DIM, t),
                pl.BlockSpec((layers, 1, SB_DIM, tm),
                             lambda i: (0, i // per_batch, 0, i % per_batch)))
    return (layers, n, SB_DIM), pl.BlockSpec((layers, tm, SB_DIM), lambda i: (0, i, 0))


def _inproj(x, g, w_in_b, li, tm):
    b, t, _ = x.shape
    n = b * t
    assert n % tm == 0
    transpose_kv = t % tm == 0 and tm % LANES == 0
    widths = (256, 256, 256, 512, 512, 256, 256, 256, 512, 512, 512)
    dtypes = (F32,) * 5 + (BF16,) * 6
    out_shape = [jax.ShapeDtypeStruct((n, w), dt) for w, dt in zip(widths, dtypes)]
    out_specs = [pl.BlockSpec((tm, w), lambda i: (i, 0)) for w in widths]
    for i, kind in zip((1, 2, 3, 4), LEAF_KINDS):
        shape, out_specs[i] = _leaf_stack(kind, 1, b, t, tm)
        out_shape[i] = jax.ShapeDtypeStruct(shape, F32)
    return pl.pallas_call(
        functools.partial(_inproj_kernel, transpose_kv=transpose_kv),
        grid=(n // tm,),
        in_specs=[
            pl.BlockSpec((tm, D_MODEL), lambda i: (i, 0)),
            pl.BlockSpec((1, D_MODEL), lambda i: (0, 0)),
            _layer_weight(w_in_b, li),
        ],
        out_specs=out_specs,
        out_shape=out_shape,
        compiler_params=_cparams(1),
        name="inproj",
    )(x.reshape(n, D_MODEL), g.reshape(1, D_MODEL), w_in_b)


def _pool_kernel(u_ref, prev_ref, buf_ref, wbd_ref, scale_ref, o_ref, xs_ref, *, tm, start_pos):
    t = pl.program_id(1)
    xs_ref[0:16, :] = jnp.where(t == 0, buf_ref[0], prev_ref[0])
    o_ref[0] = _pool_tile(u_ref[0], xs_ref, start_pos + t * tm, wbd_ref[...], scale_ref[...])


def _pool(u, buf16, wbd_b, pool_scale, tm, start_pos):
    b, t, _ = u.shape
    assert t % tm == 0 and tm % 16 == 0
    r = tm // 16
    return pl.pallas_call(
        functools.partial(_pool_kernel, tm=tm, start_pos=start_pos),
        grid=(b, t // tm),
        in_specs=[
            pl.BlockSpec((1, tm, POOL_DIM), lambda bi, ti: (bi, ti, 0)),
            pl.BlockSpec((1, 16, POOL_DIM), lambda bi, ti: (bi, jnp.maximum(ti * r - 1, 0), 0)),
            pl.BlockSpec((1, 16, POOL_DIM), lambda bi, ti: (bi, 0, 0)),
            pl.BlockSpec((POOL_DIM, POOL_DIM), lambda bi, ti: (0, 0)),
            pl.BlockSpec((1, POOL_DIM), lambda bi, ti: (0, 0)),
        ],
        out_specs=pl.BlockSpec((1, tm, POOL_DIM), lambda bi, ti: (bi, ti, 0)),
        out_shape=jax.ShapeDtypeStruct((b, t, POOL_DIM), BF16),
        scratch_shapes=[pltpu.VMEM((tm + 16, POOL_DIM), F32)],
        compiler_params=_cparams(2),
        name="pool",
    )(u, u, buf16, wbd_b, pool_scale.reshape(1, POOL_DIM))


def _suffix_ones(n):
    r = lax.broadcasted_iota(jnp.int32, (n, n), 0)
    c = lax.broadcasted_iota(jnp.int32, (n, n), 1)
    return jnp.where(r >= c, 1.0, 0.0).astype(BF16)


def _sb_kernel(q_ref, kd_ref, vd_ref, kp_ref, vp_ref, o_ref, acc_ref, ca_ref, cb_ref,
               *, tq, tk, past_blocks, past_transposed):
    qi = pl.program_id(1)
    n_pairs = SB_DIM // LANES
    walks = [_sb_pair(q_ref, kd_ref, vd_ref, kp_ref, vp_ref, acc_ref, ca_ref, cb_ref, hp, qi,
                      tq=tq, tk=tk, past_blocks=past_blocks, past_transposed=past_transposed)
             for hp in range(n_pairs)]
    for walk in walks:
        walk()
    o_ref[0] = jnp.concatenate([acc_ref[hp] for hp in range(n_pairs)], axis=1).astype(BF16)


def _sb_pair(q_ref, kd_ref, vd_ref, kp_ref, vp_ref, acc_ref, ca_ref, cb_ref, hp, qi,
             *, tq, tk, past_blocks, past_transposed):
    lanes = slice(hp * LANES, (hp + 1) * LANES)
    acc_ref, ca_ref, cb_ref = acc_ref.at[hp], ca_ref.at[hp], cb_ref.at[hp]
    lane = lax.broadcasted_iota(jnp.int32, (tq, LANES), 1)
    first = lane < SB_HEAD_DIM
    qf = q_ref[0, :, lanes].astype(F32)
    qa = jnp.where(first, qf, 0.0).astype(BF16)
    qb = jnp.where(first, 0.0, qf).astype(BF16)

    def head(qh, k, mask, ones, transposed):
        z = _dot(qh, k) if transposed else _dot_nt(qh, k)
        sp = jnp.maximum(z, 0.0) + jnp.log2(1.0 + jnp.exp2(-jnp.abs(z)))
        if mask is not None:
            sp = jnp.where(mask, sp, 0.0)
        hi = sp.astype(BF16)
        lo = (sp - hi.astype(F32)).astype(BF16)
        return z, _dot(hi, ones) + _dot(lo, ones)

    def weigh(za, csa, zb, csb, ca, cb, v, mask, transposed):
        wa = jnp.exp2(za - csa - ca)
        wb = jnp.exp2(zb - csb - cb)
        if mask is not None:
            wa = jnp.where(mask, wa, 0.0)
            wb = jnp.where(mask, wb, 0.0)
        apply = _dot_nt if transposed else _dot
        pv = jnp.where(first, apply(wa.astype(BF16), v), apply(wb.astype(BF16), v))
        return pv, ca + csa[:, 0:1], cb + csb[:, 0:1]

    def past(j):
        start = pl.multiple_of(jnp.maximum(j, 0) * tk, tk)
        if past_transposed:
            return (kp_ref[lanes, pl.ds(start, tk)].astype(BF16),
                    vp_ref[lanes, pl.ds(start, tk)].astype(BF16))
        return (kp_ref[pl.ds(start, tk), lanes].astype(BF16),
                vp_ref[pl.ds(start, tk), lanes].astype(BF16))

    n_past = past_blocks(qi)
    ones_p = _suffix_ones(tk)
    ones_o = _suffix_ones(tq)
    row = lax.broadcasted_iota(jnp.int32, (tq, tq), 0)
    col = lax.broadcasted_iota(jnp.int32, (tq, tq), 1)
    own = col < row
    have_prev = n_past > 0
    kd, vd = kd_ref[0, :, lanes], vd_ref[0, :, lanes]
    k1, v1 = past(n_past - 1)
    za, csa = head(qa, kd, own, ones_o, False)
    zb, csb = head(qb, kd, own, ones_o, False)
    za1, csa1 = head(qa, k1, have_prev, ones_p, past_transposed)
    zb1, csb1 = head(qb, k1, have_prev, ones_p, past_transposed)
    pv0, ca, cb = weigh(za, csa, zb, csb, 0.0, 0.0, vd, own, False)
    pv1, ca, cb = weigh(za1, csa1, zb1, csb1, ca, cb, v1, have_prev, past_transposed)
    acc_ref[...] = pv0 + pv1
    ca_ref[...] = ca
    cb_ref[...] = cb

    def cond(state):
        j, cm = state
        return jnp.logical_and(j >= 0, cm <= SB_ZERO_LOG * LOG2E)

    def body(state):
        j, _ = state
        k, v = past(j)
        za, csa = head(qa, k, None, ones_p, past_transposed)
        zb, csb = head(qb, k, None, ones_p, past_transposed)
        pv, ca, cb = weigh(za, csa, zb, csb, ca_ref[...], cb_ref[...], v, None, past_transposed)
        acc_ref[...] += pv
        ca_ref[...] = ca
        cb_ref[...] = cb
        return j - 1, jnp.min(jnp.minimum(ca, cb))

    cmin = jnp.min(jnp.minimum(ca, cb))
    return lambda: lax.while_loop(cond, body, (n_past - 2, cmin))


def _sb_attention(qb, kb, vb, cache, tq, tk):
    b, t, _ = qb.shape
    n_pairs = SB_DIM // LANES
    tile = pl.BlockSpec((1, tq, SB_DIM), lambda bi, qi: (bi, qi, 0))
    if cache is None:
        assert t % tq == 0 and tq % tk == 0
        past_blocks = lambda qi: qi * (tq // tk)
        full = pl.BlockSpec((None, t, SB_DIM), lambda bi, qi: (bi, 0, 0))
        k_past, v_past = kb, vb
    else:
        k_past, v_past, li = cache
        p = k_past.shape[3]
        assert t == tq and p % tk == 0
        past_blocks = lambda qi: jnp.int32(p // tk)
        full = pl.BlockSpec((None, None, SB_DIM, p), lambda bi, qi: (li, bi, 0, 0))
    return pl.pallas_call(
        functools.partial(_sb_kernel, tq=tq, tk=tk, past_blocks=past_blocks,
                          past_transposed=cache is not None),
        grid=(b, t // tq),
        in_specs=[tile, tile, tile, full, full],
        out_specs=tile,
        out_shape=jax.ShapeDtypeStruct((b, t, SB_DIM), BF16),
        scratch_shapes=[
            pltpu.VMEM((n_pairs, tq, LANES), F32),
            pltpu.VMEM((n_pairs, tq, 1), F32),
            pltpu.VMEM((n_pairs, tq, 1), F32),
        ],
        compiler_params=_cparams(2),
        name="sb_attn",
    )(qb, kb, vb, k_past, v_past)


def _exp2_rows(s, m):
    n = s.shape[1] // LANES
    if n == 0:
        return jnp.exp2(s - m[:, :s.shape[1]]).astype(BF16)
    return jnp.concatenate(
        [jnp.exp2(s[:, c * LANES:(c + 1) * LANES] - m).astype(BF16) for c in range(n)], axis=1)


def _diff_kernel(*refs, tq, tk, row_chunk, head_stride, diag_in_past, past_blocks, q_pos0,
                 lam_init):
    if diag_in_past:
        q_ref, kp_ref, vp_ref = refs[:3]
        kd_ref = vd_ref = None
        rest = refs[3:]
    else:
        q_ref, kd_ref, vd_ref, kp_ref, vp_ref = refs[:5]
        rest = refs[5:]
    (lq1_ref, lk1_ref, lq2_ref, lk2_ref, g_ref, o_ref, m_ref, a_ref, p_ref, alpha_ref) = rest
    qi = pl.program_id(2)
    lane = lax.broadcasted_iota(jnp.int32, (tq, LANES), 1)
    first = lane < DIFF_HALF_DIM
    qf = q_ref[0].astype(F32)
    q2 = jnp.concatenate([jnp.where(first, qf, 0.0), jnp.where(first, 0.0, qf)],
                         axis=0).astype(BF16)
    n_max = kp_ref.shape[0] // (tk * head_stride)
    rows = min(2 * tq, row_chunk)

    def with_ones(v):
        return jnp.concatenate([v, jnp.ones(v.shape, BF16)], axis=1)

    def past(ref, j):
        start = jnp.clip(j, 0, n_max - 1) * tk
        if head_stride == 1:
            return ref[pl.ds(pl.multiple_of(start, tk), tk), :].astype(BF16)
        first_row = start * head_stride + pl.program_id(1)
        return ref[pl.ds(first_row, tk, stride=head_stride), :].astype(BF16)

    def own_mask(r, n_rows, n_keys):
        row = (r + lax.broadcasted_iota(jnp.int32, (n_rows, n_keys), 0)) % tq
        qpos = q_pos0(qi) + row
        kpos = q_pos0(qi) + lax.broadcasted_iota(jnp.int32, (n_rows, n_keys), 1)
        return (kpos // CHUNK) <= (qpos // CHUNK)

    def step(k, v1, own, pending_own=False):
        for r in range(0, 2 * tq, rows):
            sl = slice(r, r + rows)
            if v1 is not None:
                nk = r % tq + rows if pending_own else tk
                alpha = alpha_ref[sl]
                a_ref[sl] = (jnp.concatenate([alpha, alpha], axis=1) * a_ref[sl]
                             + _dot(p_ref[sl, :nk], v1[:nk]))
            if k is not None:
                nk = r % tq + rows if own else tk
                s = _dot_nt(q2[sl], k[:nk])
                if own:
                    s = jnp.where(own_mask(r, rows, nk), s, NEG_INF)
                m_old = m_ref[sl]
                m_new = jnp.maximum(m_old, jnp.max(s, axis=-1, keepdims=True))
                alpha_ref[sl] = jnp.exp2(m_old - m_new)
                m_ref[sl] = m_new
                p_ref[sl, :nk] = _exp2_rows(s, m_new)

    n_past = past_blocks(qi)
    first_step = (pl.program_id(0) == 0) & (pl.program_id(1) == 0) & (qi == 0)
    if diag_in_past:
        m_ref[...] = jnp.full(m_ref.shape, NEG_INF, F32)

        @pl.when(first_step)
        def _():
            a_ref[...] = jnp.zeros_like(a_ref)
            alpha_ref[...] = jnp.ones_like(alpha_ref)
    else:
        s = jnp.where(own_mask(0, 2 * tq, tq), _dot_nt(q2, kd_ref[0]), NEG_INF)
        m = jnp.broadcast_to(jnp.max(s, axis=-1, keepdims=True), (2 * tq, LANES))
        m_ref[...] = m
        a_ref[...] = _dot(_exp2_rows(s, m), with_ones(vd_ref[0]))
        alpha_ref[...] = jnp.ones_like(alpha_ref)

    @pl.when(first_step)
    def _():
        p_ref[...] = jnp.zeros_like(p_ref)

    def pending_values(j):
        v1 = with_ones(past(vp_ref, j))
        return jnp.where(j >= 0, v1, jnp.zeros_like(v1))

    def body(j, carry):
        step(past(kp_ref, j), pending_values(j - 1), False)
        return carry

    lax.fori_loop(0, n_past, body, 0)
    if diag_in_past:
        step(past(kp_ref, n_past), pending_values(n_past - 1), True)
        step(None, with_ones(past(vp_ref, n_past)), False, pending_own=True)
    else:
        step(None, with_ones(past(vp_ref, n_past - 1)), False)

    lam = (jnp.exp(jnp.sum(lq1_ref[...] * lk1_ref[...], axis=-1, keepdims=True))
           - jnp.exp(jnp.sum(lq2_ref[...] * lk2_ref[...], axis=-1, keepdims=True)) + lam_init)
    o = (a_ref[:tq, :LANES] / a_ref[:tq, LANES:]
         - lam * (a_ref[tq:, :LANES] / a_ref[tq:, LANES:]))
    o_ref[0] = (_rms(o, g_ref[...]) * (1.0 - lam_init)).astype(BF16)


def _diff_attention(qb, kb, vb, cache, lam_params, diff_g, tq, tk, lam_init):
    b, t, _ = qb.shape
    tile = pl.BlockSpec((1, tq, LANES), lambda bi, hi, qi: (bi, qi, hi))
    vec64 = pl.BlockSpec((1, DIFF_HALF_DIM), lambda bi, hi, qi: (0, 0))
    vec128 = pl.BlockSpec((1, DIFF_HEAD_DIM), lambda bi, hi, qi: (0, 0))
    if cache is None:
        assert tq == tk and t % tq == 0
        past_len, head_stride = 0, 1
        past_blocks = lambda qi: qi
        full = pl.BlockSpec((None, t, LANES), lambda bi, hi, qi: (bi, 0, hi))
        kv_specs, kv_args = [full, full], (kb, vb)
    else:
        k_cache, v_cache, li = cache
        rows = k_cache.shape[2]
        past_len, head_stride = rows // DIFF_HEADS, DIFF_HEADS
        assert t == tq and past_len % tk == 0
        past_blocks = lambda qi: past_len // tk
        full = pl.BlockSpec((None, None, rows, LANES), lambda bi, hi, qi: (li, bi, 0, 0))
        kv_specs, kv_args = [tile, tile, full, full], (kb, vb, k_cache, v_cache)
    q_pos0 = lambda qi: past_len + qi * tq
    return pl.pallas_call(
        functools.partial(_diff_kernel, tq=tq, tk=tk, row_chunk=128, head_stride=head_stride,
                          diag_in_past=cache is None, past_blocks=past_blocks, q_pos0=q_pos0,
                          lam_init=lam_init),
        grid=(b, DIFF_HEADS, t // tq),
        in_specs=[tile] + kv_specs + [vec64, vec64, vec64, vec64, vec128],
        out_specs=tile,
        out_shape=jax.ShapeDtypeStruct((b, t, DIFF_DIM), BF16),
        scratch_shapes=[
            pltpu.VMEM((2 * tq, LANES), F32), pltpu.VMEM((2 * tq, 2 * LANES), F32),
            pltpu.VMEM((2 * tq, tk), BF16), pltpu.VMEM((2 * tq, LANES), F32),
        ],
        compiler_params=pltpu.CompilerParams(
            dimension_semantics=("arbitrary",) * 3, vmem_limit_bytes=VMEM_LIMIT_BYTES),
        name="diff_attn",
    )(qb, *kv_args,
      *[a.reshape(1, DIFF_HALF_DIM) for a in lam_params], diff_g.reshape(1, DIFF_HEAD_DIM))


def _memkv_kernel(m_ref, g_ref, wk_ref, wv_ref, k_ref, v_ref):
    hb = _rms(m_ref[...], g_ref[...]).astype(BF16)
    k_ref[...] = _dot(hb, wk_ref[...])
    v_ref[...] = _dot(hb, wv_ref[...])


def _memkv(mem2d, g, wk_b, wv_b, li, tm):
    n = mem2d.shape[0]
    assert n % tm == 0
    row = pl.BlockSpec((tm, D_MODEL), lambda i: (i, 0))
    return pl.pallas_call(
        _memkv_kernel,
        grid=(n // tm,),
        in_specs=[row, pl.BlockSpec((1, D_MODEL), lambda i: (0, 0)),
                  _layer_weight(wk_b, li), _layer_weight(wv_b, li)],
        out_specs=[row, row],
        out_shape=[jax.ShapeDtypeStruct((n, D_MODEL), F32)] * 2,
        compiler_params=_cparams(1),
        name="memkv",
    )(mem2d, g.reshape(1, D_MODEL), wk_b, wv_b)


def _mix_mem_kernel(x_ref, pool_ref, sb_ref, df_ref, wout_ref, gpost0_ref, gpre1_ref, gpost1_ref,
                    wq_ref, mk_ref, mv_ref, wo_ref, o_ref, att_ref, mkb_ref, mvb_ref, *, sub):
    @pl.when(pl.program_id(1) == 0)
    def _():
        for h in range(MEM_HEADS):
            sl = slice(h * MEM_HEAD_DIM, (h + 1) * MEM_HEAD_DIM)
            mkb_ref[:, sl] = mk_ref[:, h, :].astype(BF16)
            mvb_ref[:, sl] = mv_ref[:, h, :].astype(BF16)

    tiles = [slice(r, r + sub) for r in range(0, x_ref.shape[1], sub)]
    mixed = [_dot(pool_ref[0, rows, :], wout_ref[0:POOL_DIM, :])
             + _dot(sb_ref[0, rows, :], wout_ref[POOL_DIM:POOL_DIM + SB_DIM, :])
             + _dot(df_ref[0, rows, :], wout_ref[POOL_DIM + SB_DIM:, :]) for rows in tiles]
    x1 = [x_ref[0, rows, :] + _rms(mx, gpost0_ref[...]) for rows, mx in zip(tiles, mixed)]
    hb = [_rms(xi, gpre1_ref[...]).astype(BF16) for xi in x1]
    qm = [(_dot(hi, wq_ref[...]) * (MEM_HEAD_DIM ** -0.5)).astype(BF16) for hi in hb]
    for rows, qi in zip(tiles, qm):
        for h in range(MEM_HEADS):
            sl = slice(h * MEM_HEAD_DIM, (h + 1) * MEM_HEAD_DIM)
            s = _dot_nt(qi[:, sl], mkb_ref[:, sl])
            e = jnp.exp(s - jnp.max(s, axis=-1, keepdims=True))
            p = e / jnp.sum(e, axis=-1, keepdims=True)
            att_ref[rows, sl] = _dot(p.astype(BF16), mvb_ref[:, sl]).astype(BF16)
    y = [_dot(att_ref[rows, :], wo_ref[...]) for rows in tiles]
    for rows, xi, yi in zip(tiles, x1, y):
        o_ref[0, rows, :] = xi + _rms(yi, gpost1_ref[...])


def _mix_mem(x, pool_o, sb_o, df_o, wout_b, g_post0, g_pre1, g_post1, wq_b, mem, wo_b, li, tm):
    b, t, _ = x.shape
    assert t % tm == 0
    mk, mv = mem[0], mem[1]
    m = mk.shape[-3]
    tok = lambda w: pl.BlockSpec((1, tm, w), lambda bi, ti: (bi, ti, 0))
    vec = pl.BlockSpec((1, D_MODEL), lambda bi, ti: (0, 0))
    if len(mem) == 2:
        memspec = pl.BlockSpec((None, m, MEM_HEADS, MEM_HEAD_DIM), lambda bi, ti: (bi, 0, 0, 0))
    else:
        mem_li = mem[2]
        memspec = pl.BlockSpec((None, None, m, MEM_HEADS, MEM_HEAD_DIM),
                               lambda bi, ti: (mem_li, bi, 0, 0, 0))
    return pl.pallas_call(
        functools.partial(_mix_mem_kernel, sub=min(tm, 256)),
        grid=(b, t // tm),
        in_specs=[tok(D_MODEL), tok(POOL_DIM), tok(SB_DIM), tok(DIFF_DIM),
                  _layer_weight(wout_b, li), vec, vec, vec,
                  _layer_weight(wq_b, li), memspec, memspec, _layer_weight(wo_b, li)],
        out_specs=tok(D_MODEL),
        out_shape=jax.ShapeDtypeStruct((b, t, D_MODEL), F32),
        scratch_shapes=[pltpu.VMEM((tm, D_MODEL), BF16),
                        pltpu.VMEM((m, D_MODEL), BF16), pltpu.VMEM((m, D_MODEL), BF16)],
        compiler_params=pltpu.CompilerParams(
            dimension_semantics=("parallel", "arbitrary"), vmem_limit_bytes=VMEM_LIMIT_BYTES),
        name="mix_mem",
    )(x, pool_o, sb_o, df_o, wout_b, g_post0.reshape(1, -1), g_pre1.reshape(1, -1),
      g_post1.reshape(1, -1), wq_b, mk, mv, wo_b)


def _ffn_kernel(*refs, fc, sub, leaf_layers):
    x_ref, gpre_ref, gpost_ref, wg_ref, wu_ref, wd_ref = refs[:6]
    n_leaf = 4 * leaf_layers
    leaf_in, o_ref, leaf_out = refs[6:6 + n_leaf], refs[6 + n_leaf], refs[7 + n_leaf:]
    for layer in range(leaf_layers):
        for dst, src in zip(leaf_out, leaf_in[4 * layer:4 * layer + 4]):
            dst[layer] = src[0]
    for r in range(0, x_ref.shape[0], sub):
        x = x_ref[r:r + sub, :]
        hb = _rms(x, gpre_ref[...]).astype(BF16)
        acc = jnp.zeros(x.shape, F32)
        for c in range(D_FF // fc):
            sl = slice(c * fc, (c + 1) * fc)
            gate = _dot(hb, wg_ref[:, sl])
            up = _dot(hb, wu_ref[:, sl])
            act = (gate / (1.0 + jnp.exp(-gate))) * up
            acc = acc + _dot(act.astype(BF16), wd_ref[sl, :])
        o_ref[r:r + sub, :] = x + _rms(acc, gpost_ref[...])


def _ffn(x, g_pre2, g_post2, wg_b, wu_b, wd_b, li, tm, leaves=(), fc=D_FF):
    b, t, _ = x.shape
    n = b * t
    assert n % tm == 0 and D_FF % fc == 0
    row = pl.BlockSpec((tm, D_MODEL), lambda i: (i, 0))
    vec = pl.BlockSpec((1, D_MODEL), lambda i: (0, 0))
    leaf_in_specs = [_leaf_stack(kind, 1, b, t, tm)[1] for _ in leaves for kind in LEAF_KINDS]
    stacks = [_leaf_stack(kind, len(leaves), b, t, tm) for kind in LEAF_KINDS] if leaves else []
    outs = pl.pallas_call(
        functools.partial(_ffn_kernel, fc=fc, sub=min(tm, 256), leaf_layers=len(leaves)),
        grid=(n // tm,),
        in_specs=[row, vec, vec, _layer_weight(wg_b, li), _layer_weight(wu_b, li),
                  _layer_weight(wd_b, li)] + leaf_in_specs,
        out_specs=[row] + [spec for _, spec in stacks],
        out_shape=[jax.ShapeDtypeStruct((n, D_MODEL), F32)]
        + [jax.ShapeDtypeStruct(shape, F32) for shape, _ in stacks],
        compiler_params=_cparams(1),
        name="ffn",
    )(x.reshape(n, D_MODEL), g_pre2.reshape(1, -1), g_post2.reshape(1, -1), wg_b, wu_b, wd_b,
      *[a for layer in leaves for a in layer])
    return outs[0].reshape(b, t, D_MODEL), tuple(outs[1:])


def _pick(n, pref):
    return pref if n % pref == 0 else n


def _layer(x, pool_buf, sb_cache, diff_cache, mem, lw, li, lam_init, prev_leaves, last):
    (g_pre, g_post, w_in_b, w_out_b, wbd_b, pool_scale, lam_params, diff_g,
     wq_b, wo_b, wg_b, wu_b, wd_b) = lw
    b, t, _ = x.shape
    n = b * t
    past = 0 if sb_cache is None else sb_cache[0].shape[3]

    (u, sk, sv, dk, dv, sqb, skb, svb, dqb, dkb, dvb) = _inproj(
        x, g_pre[0], w_in_b, li, _pick(n, 512))
    r3 = lambda a: a.reshape(b, t, a.shape[-1])
    u, sqb, skb, svb, dqb, dkb, dvb = map(r3, (u, sqb, skb, svb, dqb, dkb, dvb))

    buf16 = jnp.concatenate([jnp.zeros((b, 1, POOL_DIM), F32), pool_buf], axis=1)
    pool_o = _pool(u, buf16, wbd_b, pool_scale, _pick(t, 512), past)
    new_pool = jnp.concatenate([pool_buf, u], axis=1)[:, -POOL_STATE:]

    if past:
        sb_o = _sb_attention(sqb, skb, svb, sb_cache, t, _pick(past, 256))
        df_o = _diff_attention(dqb, dkb, dvb, diff_cache, lam_params, diff_g,
                               t, _pick(past, 2048), lam_init)
    else:
        sb_tq = _pick(t, 256)
        sb_o = _sb_attention(sqb, skb, svb, None, sb_tq, sb_tq)
        df_tq = _pick(t, 1024)
        df_o = _diff_attention(dqb, dkb, dvb, None, lam_params, diff_g, df_tq, df_tq, lam_init)

    x = _mix_mem(x, pool_o, sb_o, df_o, w_out_b, g_post[0], g_pre[1], g_post[1],
                 wq_b, mem, wo_b, li, _pick(t, 1024))
    leaves = prev_leaves + [(sk, sv, dk, dv)]
    x, stacks = _ffn(x, g_pre[2], g_post[2], wg_b, wu_b, wd_b, li, _pick(n, 512),
                     leaves if last else ())
    return x, (stacks if last else leaves), new_pool


def _finish_leaves(leaves, b, t):
    sk, sv, dk, dv = leaves
    nl = sk.shape[0]
    if sk.ndim == 4:
        sb = lambda a: a.reshape(nl, b, SB_HEADS, SB_HEAD_DIM, t).transpose(0, 1, 4, 2, 3)
    else:
        sb = lambda a: a.reshape(nl, b, t, SB_HEADS, SB_HEAD_DIM)
    df = lambda a: a.reshape(nl, b, t, DIFF_HEADS, DIFF_HEAD_DIM)
    return sb(sk), sb(sv), df(dk), df(dv)


def _block_diag(pool_w):
    g, c, d = pool_w.shape
    out = jnp.zeros((g * c, g * d), pool_w.dtype)
    for i in range(g):
        out = lax.dynamic_update_slice(out, pool_w[i], (i * c, i * d))
    return out


def kernel(x_prompt, x_sample, cache_sb_k, cache_sb_v, cache_diff_k, cache_diff_v, cache_mem_k, cache_mem_v, state_pool, mem_prompt, g_pre, g_post, g_mem, w_in, w_out, pool_w, pool_scale, lam_q1, lam_k1, lam_q2, lam_k2, diff_g, wq_m, wk_m, wv_m, wo_m, w_gate, w_up, w_down):
    depth = w_in.shape[0]
    xp, xs = x_prompt, x_sample
    bp, mem_len = mem_prompt.shape[0], mem_prompt.shape[1]
    dshape = cache_diff_k.shape
    diff_k_rows = cache_diff_k.reshape(dshape[0], dshape[1], dshape[2] * dshape[3], dshape[4])
    diff_v_rows = cache_diff_v.reshape(dshape[0], dshape[1], dshape[2] * dshape[3], dshape[4])
    sshape = cache_sb_k.shape
    to_feature_major = lambda c: c.transpose(0, 1, 3, 4, 2).reshape(
        sshape[0], sshape[1], sshape[3] * sshape[4], sshape[2])
    sb_kT, sb_vT = to_feature_major(cache_sb_k), to_feature_major(cache_sb_v)
    (w_in_b, w_out_b, wq_b, wk_b, wv_b, wo_b, wg_b, wu_b, wd_b) = [
        w.astype(BF16) for w in (w_in, w_out, wq_m, wk_m, wv_m, wo_m, w_gate, w_up, w_down)]
    p_pool, p_mk, p_mv, s_pool = [], [], [], []
    p_leaves, s_leaves = [], []
    for li in range(depth):
        lam_init = 0.8 - 0.6 * math.exp(-0.3 * li)
        lw = (g_pre[li], g_post[li], w_in_b, w_out_b, _block_diag(pool_w[li]).astype(BF16),
              pool_scale[li], (lam_q1[li], lam_k1[li], lam_q2[li], lam_k2[li]), diff_g[li],
              wq_b, wo_b, wg_b, wu_b, wd_b)
        mk2d, mv2d = _memkv(mem_prompt.reshape(bp * mem_len, D_MODEL), g_mem[li],
                            wk_b, wv_b, li, _pick(bp * mem_len, 256))
        mk = mk2d.reshape(bp, mem_len, MEM_HEADS, MEM_HEAD_DIM)
        mv = mv2d.reshape(bp, mem_len, MEM_HEADS, MEM_HEAD_DIM)
        zero_buf = jnp.zeros((xp.shape[0], POOL_STATE, POOL_DIM), xp.dtype)
        last = li == depth - 1
        xp, p_leaves, npool = _layer(xp, zero_buf, None, None, (mk, mv), lw, li, lam_init,
                                     p_leaves, last)
        p_pool.append(npool); p_mk.append(mk); p_mv.append(mv)
        xs, s_leaves, npool = _layer(xs, state_pool[li], (sb_kT, sb_vT, li),
                                     (diff_k_rows, diff_v_rows, li),
                                     (cache_mem_k, cache_mem_v, li), lw, li, lam_init, s_leaves,
                                     last)
        s_pool.append(npool)
    return (xp, xs,
            *_finish_leaves(p_leaves, xp.shape[0], xp.shape[1]),
            jnp.stack(p_pool), jnp.stack(p_mk), jnp.stack(p_mv),
            *_finish_leaves(s_leaves, xs.shape[0], xs.shape[1]),
            jnp.stack(s_pool))
```

```python
import functools
import math

import jax
import jax.numpy as jnp
from jax import lax
from jax.experimental import pallas as pl
from jax.experimental.pallas import tpu as pltpu

F32 = jnp.float32
BF16 = jnp.bfloat16

D_MODEL = 1024
CHUNK = 64
POOL_WINDOWS = (2, 4, 8, 16)
POOL_GROUP_DIM = 64
POOL_DIM = 256
POOL_STATE = 15
SB_HEADS = 4
SB_HEAD_DIM = 64
SB_DIM = 256
DIFF_HEADS = 4
DIFF_HALF_DIM = 64
DIFF_HEAD_DIM = 128
DIFF_DIM = 512
MEM_HEADS = 4
MEM_HEAD_DIM = 256
D_FF = 2816
EPS = 1e-6
NEG_INF = -1e30
LOG2E = 1.4426950408889634

LANES = 128
MXU_DIM = 256
VMEM_LIMIT_BYTES = 56 * 1024 * 1024

SB_ZERO_LOG = 110.0


def _cparams(n_grid):
    return pltpu.CompilerParams(
        dimension_semantics=("parallel",) * n_grid,
        vmem_limit_bytes=VMEM_LIMIT_BYTES,
    )


def _rms(x, g):
    return x * lax.rsqrt(jnp.mean(x * x, axis=-1, keepdims=True) + EPS) * g


def _dot(a, b):
    return jnp.dot(a, b, preferred_element_type=F32)


def _dot_nt(a, b):
    return lax.dot_general(a, b, (((1,), (1,)), ((), ())), preferred_element_type=F32)


def _pool_tile(u, xs_ref, pos0, wbd, scale):
    tm = u.shape[0]
    xs_ref[16:, :] = u
    pos = pos0 + lax.broadcasted_iota(jnp.int32, (tm, POOL_DIM), 0)
    group = lax.broadcasted_iota(jnp.int32, (tm, POOL_DIM), 1) // POOL_GROUP_DIM
    mean = jnp.zeros((tm, POOL_DIM), F32)
    assert POOL_WINDOWS == (2, 4, 8, 16)
    run = xs_ref[...]
    for gi, w in enumerate(POOL_WINDOWS):
        run = run + pltpu.roll(run, w // 2, 0)
        cnt = jnp.minimum(w, pos + 1).astype(F32)
        mean = jnp.where(group == gi, run[16:] / cnt, mean)
    d = (mean - u).astype(BF16)
    return (_dot(d, wbd) * scale).astype(BF16)


def _inproj_kernel(x_ref, g_ref, w_ref,
                   u_ref, sk_ref, sv_ref, dk_ref, dv_ref,
                   sqb_ref, skb_ref, svb_ref, dqb_ref, dkb_ref, dvb_ref, *, transpose_kv):
    tm = x_ref.shape[0]
    sub = min(tm, 256)
    tiles = [slice(r, r + sub) for r in range(0, tm, sub)]
    hbs = [_rms(x_ref[rows, :], g_ref[...]).astype(BF16) for rows in tiles]
    for rows, hb in zip(tiles, hbs):
        r = rows.start

        def cols(lo, hi):
            return _dot(hb, w_ref[:, lo:hi])

        u_ref[rows, :] = cols(0, 256)
        sqb_ref[rows, :] = (cols(256, 512) * (LOG2E * SB_HEAD_DIM ** -0.5)).astype(BF16)
        sk = cols(512, 768)
        skb_ref[rows, :] = sk.astype(BF16)
        sv = cols(768, 1024)
        svb_ref[rows, :] = sv.astype(BF16)
        if transpose_kv:
            sk_ref[0, 0, :, rows] = sk.T
            sv_ref[0, 0, :, rows] = sv.T
        else:
            sk_ref[0, rows, :] = sk
            sv_ref[0, rows, :] = sv
        dqb_ref[rows, :] = (cols(1024, 1536) * (LOG2E * DIFF_HALF_DIM ** -0.5)).astype(BF16)
        dk = cols(1536, 2048)
        dkb_ref[rows, :] = dk.astype(BF16)
        dv = cols(2048, 2560)
        dvb_ref[rows, :] = dv.astype(BF16)
        for h in range(DIFF_HEADS):
            sl = slice(h * DIFF_HEAD_DIM, (h + 1) * DIFF_HEAD_DIM)
            dk_ref[0, pl.ds(r * DIFF_HEADS + h, sub, stride=DIFF_HEADS), :] = dk[:, sl]
            dv_ref[0, pl.ds(r * DIFF_HEADS + h, sub, stride=DIFF_HEADS), :] = dv[:, sl]


def _layer_weight(w, li):
    return pl.BlockSpec((None,) + w.shape[1:], lambda *_: (li, 0, 0),
                        pipeline_mode=pl.Buffered(1))


LEAF_KINDS = ("sb", "sb", "diff", "diff")


def _leaf_stack(kind, layers, b, t, tm):
    n = b * t
    if kind == "diff":
        return ((layers, n * DIFF_HEADS, DIFF_HEAD_DIM),
                pl.BlockSpec((layers, tm * DIFF_HEADS, DIFF_HEAD_DIM), lambda i: (0, i, 0)))
    if t % tm == 0 and tm % LANES == 0:
        per_batch = t // tm
        return ((layers, b, SB_DIM, t),
                pl.BlockSpec((layers, 1, SB_DIM, tm),
                             lambda i: (0, i // per_batch, 0, i % per_batch)))
    return (layers, n, SB_DIM), pl.BlockSpec((layers, tm, SB_DIM), lambda i: (0, i, 0))


def _inproj(x, g, w_in_b, li, tm):
    b, t, _ = x.shape
    n = b * t
    assert n % tm == 0
    transpose_kv = t % tm == 0 and tm % LANES == 0
    widths = (256, 256, 256, 512, 512, 256, 256, 256, 512, 512, 512)
    dtypes = (F32,) * 5 + (BF16,) * 6
    out_shape = [jax.ShapeDtypeStruct((n, w), dt) for w, dt in zip(widths, dtypes)]
    out_specs = [pl.BlockSpec((tm, w), lambda i: (i, 0)) for w in widths]
    for i, kind in zip((1, 2, 3, 4), LEAF_KINDS):
        shape, out_specs[i] = _leaf_stack(kind, 1, b, t, tm)
        out_shape[i] = jax.ShapeDtypeStruct(shape, F32)
    return pl.pallas_call(
        functools.partial(_inproj_kernel, transpose_kv=transpose_kv),
        grid=(n // tm,),
        in_specs=[
            pl.BlockSpec((tm, D_MODEL), lambda i: (i, 0)),
            pl.BlockSpec((1, D_MODEL), lambda i: (0, 0)),
            _layer_weight(w_in_b, li),
        ],
        out_specs=out_specs,
        out_shape=out_shape,
        compiler_params=_cparams(1),
        name="inproj",
    )(x.reshape(n, D_MODEL), g.reshape(1, D_MODEL), w_in_b)


def _pool_kernel(u_ref, prev_ref, buf_ref, wbd_ref, scale_ref, o_ref, xs_ref, *, tm, start_pos):
    t = pl.program_id(1)
    xs_ref[0:16, :] = jnp.where(t == 0, buf_ref[0], prev_ref[0])
    o_ref[0] = _pool_tile(u_ref[0], xs_ref, start_pos + t * tm, wbd_ref[...], scale_ref[...])


def _pool(u, buf16, wbd_b, pool_scale, tm, start_pos):
    b, t, _ = u.shape
    assert t % tm == 0 and tm % 16 == 0
    r = tm // 16
    return pl.pallas_call(
        functools.partial(_pool_kernel, tm=tm, start_pos=start_pos),
        grid=(b, t // tm),
        in_specs=[
            pl.BlockSpec((1, tm, POOL_DIM), lambda bi, ti: (bi, ti, 0)),
            pl.BlockSpec((1, 16, POOL_DIM), lambda bi, ti: (bi, jnp.maximum(ti * r - 1, 0), 0)),
            pl.BlockSpec((1, 16, POOL_DIM), lambda bi, ti: (bi, 0, 0)),
            pl.BlockSpec((POOL_DIM, POOL_DIM), lambda bi, ti: (0, 0)),
            pl.BlockSpec((1, POOL_DIM), lambda bi, ti: (0, 0)),
        ],
        out_specs=pl.BlockSpec((1, tm, POOL_DIM), lambda bi, ti: (bi, ti, 0)),
        out_shape=jax.ShapeDtypeStruct((b, t, POOL_DIM), BF16),
        scratch_shapes=[pltpu.VMEM((tm + 16, POOL_DIM), F32)],
        compiler_params=_cparams(2),
        name="pool",
    )(u, u, buf16, wbd_b, pool_scale.reshape(1, POOL_DIM))


def _suffix_ones(n):
    r = lax.broadcasted_iota(jnp.int32, (n, n), 0)
    c = lax.broadcasted_iota(jnp.int32, (n, n), 1)
    return jnp.where(r >= c, 1.0, 0.0).astype(BF16)


def _sb_kernel(q_ref, kd_ref, vd_ref, kp_ref, vp_ref, o_ref, acc_ref, ca_ref, cb_ref,
               *, tq, tk, past_blocks, past_transposed):
    qi = pl.program_id(1)
    n_pairs = SB_DIM // LANES
    walks = [_sb_pair(q_ref, kd_ref, vd_ref, kp_ref, vp_ref, acc_ref, ca_ref, cb_ref, hp, qi,
                      tq=tq, tk=tk, past_blocks=past_blocks, past_transposed=past_transposed)
             for hp in range(n_pairs)]
    for walk in walks:
        walk()
    o_ref[0] = jnp.concatenate([acc_ref[hp] for hp in range(n_pairs)], axis=1).astype(BF16)


def _sb_pair(q_ref, kd_ref, vd_ref, kp_ref, vp_ref, acc_ref, ca_ref, cb_ref, hp, qi,
             *, tq, tk, past_blocks, past_transposed):
    lanes = slice(hp * LANES, (hp + 1) * LANES)
    acc_ref, ca_ref, cb_ref = acc_ref.at[hp], ca_ref.at[hp], cb_ref.at[hp]
    lane = lax.broadcasted_iota(jnp.int32, (tq, LANES), 1)
    first = lane < SB_HEAD_DIM
    qf = q_ref[0, :, lanes].astype(F32)
    qa = jnp.where(first, qf, 0.0).astype(BF16)
    qb = jnp.where(first, 0.0, qf).astype(BF16)

    def head(qh, k, mask, ones, transposed):
        z = _dot(qh, k) if transposed else _dot_nt(qh, k)
        sp = jnp.maximum(z, 0.0) + jnp.log2(1.0 + jnp.exp2(-jnp.abs(z)))
        if mask is not None:
            sp = jnp.where(mask, sp, 0.0)
        hi = sp.astype(BF16)
        lo = (sp - hi.astype(F32)).astype(BF16)
        return z, _dot(hi, ones) + _dot(lo, ones)

    def weigh(za, csa, zb, csb, ca, cb, v, mask, transposed):
        wa = jnp.exp2(za - csa - ca)
        wb = jnp.exp2(zb - csb - cb)
        if mask is not None:
            wa = jnp.where(mask, wa, 0.0)
            wb = jnp.where(mask, wb, 0.0)
        apply = _dot_nt if transposed else _dot
        pv = jnp.where(first, apply(wa.astype(BF16), v), apply(wb.astype(BF16), v))
        return pv, ca + csa[:, 0:1], cb + csb[:, 0:1]

    def past(j):
        start = pl.multiple_of(jnp.maximum(j, 0) * tk, tk)
        if past_transposed:
            return (kp_ref[lanes, pl.ds(start, tk)].astype(BF16),
                    vp_ref[lanes, pl.ds(start, tk)].astype(BF16))
        return (kp_ref[pl.ds(start, tk), lanes].astype(BF16),
                vp_ref[pl.ds(start, tk), lanes].astype(BF16))

    n_past = past_blocks(qi)
    ones_p = _suffix_ones(tk)
    ones_o = _suffix_ones(tq)
    row = lax.broadcasted_iota(jnp.int32, (tq, tq), 0)
    col = lax.broadcasted_iota(jnp.int32, (tq, tq), 1)
    own = col < row
    have_prev = n_past > 0
    kd, vd = kd_ref[0, :, lanes], vd_ref[0, :, lanes]
    k1, v1 = past(n_past - 1)
    za, csa = head(qa, kd, own, ones_o, False)
    zb, csb = head(qb, kd, own, ones_o, False)
    za1, csa1 = head(qa, k1, have_prev, ones_p, past_transposed)
    zb1, csb1 = head(qb, k1, have_prev, ones_p, past_transposed)
    pv0, ca, cb = weigh(za, csa, zb, csb, 0.0, 0.0, vd, own, False)
    pv1, ca, cb = weigh(za1, csa1, zb1, csb1, ca, cb, v1, have_prev, past_transposed)
    acc_ref[...] = pv0 + pv1
    ca_ref[...] = ca
    cb_ref[...] = cb

    def cond(state):
        j, cm = state
        return jnp.logical_and(j >= 0, cm <= SB_ZERO_LOG * LOG2E)

    def body(state):
        j, _ = state
        k, v = past(j)
        za, csa = head(qa, k, None, ones_p, past_transposed)
        zb, csb = head(qb, k, None, ones_p, past_transposed)
        pv, ca, cb = weigh(za, csa, zb, csb, ca_ref[...], cb_ref[...], v, None, past_transposed)
        acc_ref[...] += pv
        ca_ref[...] = ca
        cb_ref[...] = cb
        return j - 1, jnp.min(jnp.minimum(ca, cb))

    cmin = jnp.min(jnp.minimum(ca, cb))
    return lambda: lax.while_loop(cond, body, (n_past - 2, cmin))


def _sb_attention(qb, kb, vb, cache, tq, tk):
    b, t, _ = qb.shape
    n_pairs = SB_DIM // LANES
    tile = pl.BlockSpec((1, tq, SB_DIM), lambda bi, qi: (bi, qi, 0))
    if cache is None:
        assert t % tq == 0 and tq % tk == 0
        past_blocks = lambda qi: qi * (tq // tk)
        full = pl.BlockSpec((None, t, SB_DIM), lambda bi, qi: (bi, 0, 0))
        k_past, v_past = kb, vb
    else:
        k_past, v_past, li = cache
        p = k_past.shape[3]
        assert t == tq and p % tk == 0
        past_blocks = lambda qi: jnp.int32(p // tk)
        full = pl.BlockSpec((None, None, SB_DIM, p), lambda bi, qi: (li, bi, 0, 0))
    return pl.pallas_call(
        functools.partial(_sb_kernel, tq=tq, tk=tk, past_blocks=past_blocks,
                          past_transposed=cache is not None),
        grid=(b, t // tq),
        in_specs=[tile, tile, tile, full, full],
        out_specs=tile,
        out_shape=jax.ShapeDtypeStruct((b, t, SB_DIM), BF16),
        scratch_shapes=[
            pltpu.VMEM((n_pairs, tq, LANES), F32),
            pltpu.VMEM((n_pairs, tq, 1), F32),
            pltpu.VMEM((n_pairs, tq, 1), F32),
        ],
        compiler_params=_cparams(2),
        name="sb_attn",
    )(qb, kb, vb, k_past, v_past)


def _exp2_rows(s, m):
    n = s.shape[1] // LANES
    if n == 0:
        return jnp.exp2(s - m[:, :s.shape[1]]).astype(BF16)
    return jnp.concatenate(
        [jnp.exp2(s[:, c * LANES:(c + 1) * LANES] - m).astype(BF16) for c in range(n)], axis=1)


def _diff_kernel(*refs, tq, tk, row_chunk, head_stride, diag_in_past, past_blocks, q_pos0,
                 lam_init):
    if diag_in_past:
        q_ref, kp_ref, vp_ref = refs[:3]
        kd_ref = vd_ref = None
        rest = refs[3:]
    else:
        q_ref, kd_ref, vd_ref, kp_ref, vp_ref = refs[:5]
        rest = refs[5:]
    (lq1_ref, lk1_ref, lq2_ref, lk2_ref, g_ref, o_ref, m_ref, a_ref, p_ref, alpha_ref) = rest
    qi = pl.program_id(2)
    lane = lax.broadcasted_iota(jnp.int32, (tq, LANES), 1)
    first = lane < DIFF_HALF_DIM
    qf = q_ref[0].astype(F32)
    q2 = jnp.concatenate([jnp.where(first, qf, 0.0), jnp.where(first, 0.0, qf)],
                         axis=0).astype(BF16)
    n_max = kp_ref.shape[0] // (tk * head_stride)
    rows = min(2 * tq, row_chunk)

    def with_ones(v):
        return jnp.concatenate([v, jnp.ones(v.shape, BF16)], axis=1)

    def past(ref, j):
        start = jnp.clip(j, 0, n_max - 1) * tk
        if head_stride == 1:
            return ref[pl.ds(pl.multiple_of(start, tk), tk), :].astype(BF16)
        first_row = start * head_stride + pl.program_id(1)
        return ref[pl.ds(first_row, tk, stride=head_stride), :].astype(BF16)

    def own_mask(r, n_rows, n_keys):
        row = (r + lax.broadcasted_iota(jnp.int32, (n_rows, n_keys), 0)) % tq
        qpos = q_pos0(qi) + row
        kpos = q_pos0(qi) + lax.broadcasted_iota(jnp.int32, (n_rows, n_keys), 1)
        return (kpos // CHUNK) <= (qpos // CHUNK)

    def step(k, v1, own, pending_own=False):
        for r in range(0, 2 * tq, rows):
            sl = slice(r, r + rows)
            if v1 is not None:
                nk = r % tq + rows if pending_own else tk
                alpha = alpha_ref[sl]
                a_ref[sl] = (jnp.concatenate([alpha, alpha], axis=1) * a_ref[sl]
                             + _dot(p_ref[sl, :nk], v1[:nk]))
            if k is not None:
                nk = r % tq + rows if own else tk
                s = _dot_nt(q2[sl], k[:nk])
                if own:
                    s = jnp.where(own_mask(r, rows, nk), s, NEG_INF)
                m_old = m_ref[sl]
                m_new = jnp.maximum(m_old, jnp.max(s, axis=-1, keepdims=True))
                alpha_ref[sl] = jnp.exp2(m_old - m_new)
                m_ref[sl] = m_new
                p_ref[sl, :nk] = _exp2_rows(s, m_new)

    n_past = past_blocks(qi)
    first_step = (pl.program_id(0) == 0) & (pl.program_id(1) == 0) & (qi == 0)
    if diag_in_past:
        m_ref[...] = jnp.full(m_ref.shape, NEG_INF, F32)

        @pl.when(first_step)
        def _():
            a_ref[...] = jnp.zeros_like(a_ref)
            alpha_ref[...] = jnp.ones_like(alpha_ref)
    else:
        s = jnp.where(own_mask(0, 2 * tq, tq), _dot_nt(q2, kd_ref[0]), NEG_INF)
        m = jnp.broadcast_to(jnp.max(s, axis=-1, keepdims=True), (2 * tq, LANES))
        m_ref[...] = m
        a_ref[...] = _dot(_exp2_rows(s, m), with_ones(vd_ref[0]))
        alpha_ref[...] = jnp.ones_like(alpha_ref)

    @pl.when(first_step)
    def _():
        p_ref[...] = jnp.zeros_like(p_ref)

    def pending_values(j):
        v1 = with_ones(past(vp_ref, j))
        return jnp.where(j >= 0, v1, jnp.zeros_like(v1))

    def body(j, carry):
        step(past(kp_ref, j), pending_values(j - 1), False)
        return carry

    lax.fori_loop(0, n_past, body, 0)
    if diag_in_past:
        step(past(kp_ref, n_past), pending_values(n_past - 1), True)
        step(None, with_ones(past(vp_ref, n_past)), False, pending_own=True)
    else:
        step(None, with_ones(past(vp_ref, n_past - 1)), False)

    lam = (jnp.exp(jnp.sum(lq1_ref[...] * lk1_ref[...], axis=-1, keepdims=True))
           - jnp.exp(jnp.sum(lq2_ref[...] * lk2_ref[...], axis=-1, keepdims=True)) + lam_init)
    o = (a_ref[:tq, :LANES] / a_ref[:tq, LANES:]
         - lam * (a_ref[tq:, :LANES] / a_ref[tq:, LANES:]))
    o_ref[0] = (_rms(o, g_ref[...]) * (1.0 - lam_init)).astype(BF16)


def _diff_attention(qb, kb, vb, cache, lam_params, diff_g, tq, tk, lam_init):
    b, t, _ = qb.shape
    tile = pl.BlockSpec((1, tq, LANES), lambda bi, hi, qi: (bi, qi, hi))
    vec64 = pl.BlockSpec((1, DIFF_HALF_DIM), lambda bi, hi, qi: (0, 0))
    vec128 = pl.BlockSpec((1, DIFF_HEAD_DIM), lambda bi, hi, qi: (0, 0))
    if cache is None:
        assert tq == tk and t % tq == 0
        past_len, head_stride = 0, 1
        past_blocks = lambda qi: qi
        full = pl.BlockSpec((None, t, LANES), lambda bi, hi, qi: (bi, 0, hi))
        kv_specs, kv_args = [full, full], (kb, vb)
    else:
        k_cache, v_cache, li = cache
        rows = k_cache.shape[2]
        past_len, head_stride = rows // DIFF_HEADS, DIFF_HEADS
        assert t == tq and past_len % tk == 0
        past_blocks = lambda qi: past_len // tk
        full = pl.BlockSpec((None, None, rows, LANES), lambda bi, hi, qi: (li, bi, 0, 0))
        kv_specs, kv_args = [tile, tile, full, full], (kb, vb, k_cache, v_cache)
    q_pos0 = lambda qi: past_len + qi * tq
    return pl.pallas_call(
        functools.partial(_diff_kernel, tq=tq, tk=tk, row_chunk=128, head_stride=head_stride,
                          diag_in_past=cache is None, past_blocks=past_blocks, q_pos0=q_pos0,
                          lam_init=lam_init),
        grid=(b, DIFF_HEADS, t // tq),
        in_specs=[tile] + kv_specs + [vec64, vec64, vec64, vec64, vec128],
        out_specs=tile,
        out_shape=jax.ShapeDtypeStruct((b, t, DIFF_DIM), BF16),
        scratch_shapes=[
            pltpu.VMEM((2 * tq, LANES), F32), pltpu.VMEM((2 * tq, 2 * LANES), F32),
            pltpu.VMEM((2 * tq, tk), BF16), pltpu.VMEM((2 * tq, LANES), F32),
        ],
        compiler_params=pltpu.CompilerParams(
            dimension_semantics=("arbitrary",) * 3, vmem_limit_bytes=VMEM_LIMIT_BYTES),
        name="diff_attn",
    )(qb, *kv_args,
      *[a.reshape(1, DIFF_HALF_DIM) for a in lam_params], diff_g.reshape(1, DIFF_HEAD_DIM))


def _memkv_kernel(m_ref, g_ref, wk_ref, wv_ref, k_ref, v_ref):
    hb = _rms(m_ref[...], g_ref[...]).astype(BF16)
    k_ref[...] = _dot(hb, wk_ref[...])
    v_ref[...] = _dot(hb, wv_ref[...])


def _memkv(mem2d, g, wk_b, wv_b, li, tm):
    n = mem2d.shape[0]
    assert n % tm == 0
    row = pl.BlockSpec((tm, D_MODEL), lambda i: (i, 0))
    return pl.pallas_call(
        _memkv_kernel,
        grid=(n // tm,),
        in_specs=[row, pl.BlockSpec((1, D_MODEL), lambda i: (0, 0)),
                  _layer_weight(wk_b, li), _layer_weight(wv_b, li)],
        out_specs=[row, row],
        out_shape=[jax.ShapeDtypeStruct((n, D_MODEL), F32)] * 2,
        compiler_params=_cparams(1),
        name="memkv",
    )(mem2d, g.reshape(1, D_MODEL), wk_b, wv_b)


def _mix_mem_kernel(x_ref, pool_ref, sb_ref, df_ref, wout_ref, gpost0_ref, gpre1_ref, gpost1_ref,
                    wq_ref, mk_ref, mv_ref, wo_ref, o_ref, att_ref, mkb_ref, mvb_ref, *, sub):
    @pl.when(pl.program_id(1) == 0)
    def _():
        for h in range(MEM_HEADS):
            sl = slice(h * MEM_HEAD_DIM, (h + 1) * MEM_HEAD_DIM)
            mkb_ref[:, sl] = mk_ref[:, h, :].astype(BF16)
            mvb_ref[:, sl] = mv_ref[:, h, :].astype(BF16)

    tiles = [slice(r, r + sub) for r in range(0, x_ref.shape[1], sub)]
    mixed = [_dot(pool_ref[0, rows, :], wout_ref[0:POOL_DIM, :])
             + _dot(sb_ref[0, rows, :], wout_ref[POOL_DIM:POOL_DIM + SB_DIM, :])
             + _dot(df_ref[0, rows, :], wout_ref[POOL_DIM + SB_DIM:, :]) for rows in tiles]
    x1 = [x_ref[0, rows, :] + _rms(mx, gpost0_ref[...]) for rows, mx in zip(tiles, mixed)]
    hb = [_rms(xi, gpre1_ref[...]).astype(BF16) for xi in x1]
    qm = [(_dot(hi, wq_ref[...]) * (MEM_HEAD_DIM ** -0.5)).astype(BF16) for hi in hb]
    for rows, qi in zip(tiles, qm):
        for h in range(MEM_HEADS):
            sl = slice(h * MEM_HEAD_DIM, (h + 1) * MEM_HEAD_DIM)
            s = _dot_nt(qi[:, sl], mkb_ref[:, sl])
            e = jnp.exp(s - jnp.max(s, axis=-1, keepdims=True))
            p = e / jnp.sum(e, axis=-1, keepdims=True)
            att_ref[rows, sl] = _dot(p.astype(BF16), mvb_ref[:, sl]).astype(BF16)
    y = [_dot(att_ref[rows, :], wo_ref[...]) for rows in tiles]
    for rows, xi, yi in zip(tiles, x1, y):
        o_ref[0, rows, :] = xi + _rms(yi, gpost1_ref[...])


def _mix_mem(x, pool_o, sb_o, df_o, wout_b, g_post0, g_pre1, g_post1, wq_b, mem, wo_b, li, tm):
    b, t, _ = x.shape
    assert t % tm == 0
    mk, mv = mem[0], mem[1]
    m = mk.shape[-3]
    tok = lambda w: pl.BlockSpec((1, tm, w), lambda bi, ti: (bi, ti, 0))
    vec = pl.BlockSpec((1, D_MODEL), lambda bi, ti: (0, 0))
    if len(mem) == 2:
        memspec = pl.BlockSpec((None, m, MEM_HEADS, MEM_HEAD_DIM), lambda bi, ti: (bi, 0, 0, 0))
    else:
        mem_li = mem[2]
        memspec = pl.BlockSpec((None, None, m, MEM_HEADS, MEM_HEAD_DIM),
                               lambda bi, ti: (mem_li, bi, 0, 0, 0))
    return pl.pallas_call(
        functools.partial(_mix_mem_kernel, sub=min(tm, 256)),
        grid=(b, t // tm),
        in_specs=[tok(D_MODEL), tok(POOL_DIM), tok(SB_DIM), tok(DIFF_DIM),
                  _layer_weight(wout_b, li), vec, vec, vec,
                  _layer_weight(wq_b, li), memspec, memspec, _layer_weight(wo_b, li)],
        out_specs=tok(D_MODEL),
        out_shape=jax.ShapeDtypeStruct((b, t, D_MODEL), F32),
        scratch_shapes=[pltpu.VMEM((tm, D_MODEL), BF16),
                        pltpu.VMEM((m, D_MODEL), BF16), pltpu.VMEM((m, D_MODEL), BF16)],
        compiler_params=pltpu.CompilerParams(
            dimension_semantics=("parallel", "arbitrary"), vmem_limit_bytes=VMEM_LIMIT_BYTES),
        name="mix_mem",
    )(x, pool_o, sb_o, df_o, wout_b, g_post0.reshape(1, -1), g_pre1.reshape(1, -1),
      g_post1.reshape(1, -1), wq_b, mk, mv, wo_b)


def _ffn_kernel(*refs, fc, sub, leaf_layers):
    x_ref, gpre_ref, gpost_ref, wg_ref, wu_ref, wd_ref = refs[:6]
    n_leaf = 4 * leaf_layers
    leaf_in, o_ref, leaf_out = refs[6:6 + n_leaf], refs[6 + n_leaf], refs[7 + n_leaf:]
    for layer in range(leaf_layers):
        for dst, src in zip(leaf_out, leaf_in[4 * layer:4 * layer + 4]):
            dst[layer] = src[0]
    for r in range(0, x_ref.shape[0], sub):
        x = x_ref[r:r + sub, :]
        hb = _rms(x, gpre_ref[...]).astype(BF16)
        acc = jnp.zeros(x.shape, F32)
        for c in range(D_FF // fc):
            sl = slice(c * fc, (c + 1) * fc)
            gate = _dot(hb, wg_ref[:, sl])
            up = _dot(hb, wu_ref[:, sl])
            act = (gate / (1.0 + jnp.exp(-gate))) * up
            acc = acc + _dot(act.astype(BF16), wd_ref[sl, :])
        o_ref[r:r + sub, :] = x + _rms(acc, gpost_ref[...])


def _ffn(x, g_pre2, g_post2, wg_b, wu_b, wd_b, li, tm, leaves=(), fc=D_FF):
    b, t, _ = x.shape
    n = b * t
    assert n % tm == 0 and D_FF % fc == 0
    row = pl.BlockSpec((tm, D_MODEL), lambda i: (i, 0))
    vec = pl.BlockSpec((1, D_MODEL), lambda i: (0, 0))
    leaf_in_specs = [_leaf_stack(kind, 1, b, t, tm)[1] for _ in leaves for kind in LEAF_KINDS]
    stacks = [_leaf_stack(kind, len(leaves), b, t, tm) for kind in LEAF_KINDS] if leaves else []
    outs = pl.pallas_call(
        functools.partial(_ffn_kernel, fc=fc, sub=min(tm, 256), leaf_layers=len(leaves)),
        grid=(n // tm,),
        in_specs=[row, vec, vec, _layer_weight(wg_b, li), _layer_weight(wu_b, li),
                  _layer_weight(wd_b, li)] + leaf_in_specs,
        out_specs=[row] + [spec for _, spec in stacks],
        out_shape=[jax.ShapeDtypeStruct((n, D_MODEL), F32)]
        + [jax.ShapeDtypeStruct(shape, F32) for shape, _ in stacks],
        compiler_params=_cparams(1),
        name="ffn",
    )(x.reshape(n, D_MODEL), g_pre2.reshape(1, -1), g_post2.reshape(1, -1), wg_b, wu_b, wd_b,
      *[a for layer in leaves for a in layer])
    return outs[0].reshape(b, t, D_MODEL), tuple(outs[1:])


def _pick(n, pref):
    return pref if n % pref == 0 else n


def _layer(x, pool_buf, sb_cache, diff_cache, mem, lw, li, lam_init, prev_leaves, last):
    (g_pre, g_post, w_in_b, w_out_b, wbd_b, pool_scale, lam_params, diff_g,
     wq_b, wo_b, wg_b, wu_b, wd_b) = lw
    b, t, _ = x.shape
    n = b * t
    past = 0 if sb_cache is None else sb_cache[0].shape[3]

    (u, sk, sv, dk, dv, sqb, skb, svb, dqb, dkb, dvb) = _inproj(
        x, g_pre[0], w_in_b, li, _pick(n, 1024))
    r3 = lambda a: a.reshape(b, t, a.shape[-1])
    u, sqb, skb, svb, dqb, dkb, dvb = map(r3, (u, sqb, skb, svb, dqb, dkb, dvb))

    buf16 = jnp.concatenate([jnp.zeros((b, 1, POOL_DIM), F32), pool_buf], axis=1)
    pool_o = _pool(u, buf16, wbd_b, pool_scale, _pick(t, 1024), past)
    new_pool = jnp.concatenate([pool_buf, u], axis=1)[:, -POOL_STATE:]

    if past:
        sb_o = _sb_attention(sqb, skb, svb, sb_cache, t, _pick(past, 256))
        df_o = _diff_attention(dqb, dkb, dvb, diff_cache, lam_params, diff_g,
                               t, _pick(past, 2048), lam_init)
    else:
        sb_tq = _pick(t, 256)
        sb_o = _sb_attention(sqb, skb, svb, None, sb_tq, sb_tq)
        df_tq = _pick(t, 1024)
        df_o = _diff_attention(dqb, dkb, dvb, None, lam_params, diff_g, df_tq, df_tq, lam_init)

    x = _mix_mem(x, pool_o, sb_o, df_o, w_out_b, g_post[0], g_pre[1], g_post[1],
                 wq_b, mem, wo_b, li, _pick(t, 1024))
    leaves = prev_leaves + [(sk, sv, dk, dv)]
    x, stacks = _ffn(x, g_pre[2], g_post[2], wg_b, wu_b, wd_b, li, _pick(n, 512),
                     leaves if last else ())
    return x, (stacks if last else leaves), new_pool


def _finish_leaves(leaves, b, t):
    sk, sv, dk, dv = leaves
    nl = sk.shape[0]
    if sk.ndim == 4:
        sb = lambda a: a.reshape(nl, b, SB_HEADS, SB_HEAD_DIM, t).transpose(0, 1, 4, 2, 3)
    else:
        sb = lambda a: a.reshape(nl, b, t, SB_HEADS, SB_HEAD_DIM)
    df = lambda a: a.reshape(nl, b, t, DIFF_HEADS, DIFF_HEAD_DIM)
    return sb(sk), sb(sv), df(dk), df(dv)


def _block_diag(pool_w):
    g, c, d = pool_w.shape
    out = jnp.zeros((g * c, g * d), pool_w.dtype)
    for i in range(g):
        out = lax.dynamic_update_slice(out, pool_w[i], (i * c, i * d))
    return out


def kernel(x_prompt, x_sample, cache_sb_k, cache_sb_v, cache_diff_k, cache_diff_v, cache_mem_k, cache_mem_v, state_pool, mem_prompt, g_pre, g_post, g_mem, w_in, w_out, pool_w, pool_scale, lam_q1, lam_k1, lam_q2, lam_k2, diff_g, wq_m, wk_m, wv_m, wo_m, w_gate, w_up, w_down):
    depth = w_in.shape[0]
    xp, xs = x_prompt, x_sample
    bp, mem_len = mem_prompt.shape[0], mem_prompt.shape[1]
    dshape = cache_diff_k.shape
    diff_k_rows = cache_diff_k.reshape(dshape[0], dshape[1], dshape[2] * dshape[3], dshape[4])
    diff_v_rows = cache_diff_v.reshape(dshape[0], dshape[1], dshape[2] * dshape[3], dshape[4])
    sshape = cache_sb_k.shape
    to_feature_major = lambda c: c.transpose(0, 1, 3, 4, 2).reshape(
        sshape[0], sshape[1], sshape[3] * sshape[4], sshape[2])
    sb_kT, sb_vT = to_feature_major(cache_sb_k), to_feature_major(cache_sb_v)
    (w_in_b, w_out_b, wq_b, wk_b, wv_b, wo_b, wg_b, wu_b, wd_b) = [
        w.astype(BF16) for w in (w_in, w_out, wq_m, wk_m, wv_m, wo_m, w_gate, w_up, w_down)]
    p_pool, p_mk, p_mv, s_pool = [], [], [], []
    p_leaves, s_leaves = [], []
    for li in range(depth):
        lam_init = 0.8 - 0.6 * math.exp(-0.3 * li)
        lw = (g_pre[li], g_post[li], w_in_b, w_out_b, _block_diag(pool_w[li]).astype(BF16),
              pool_scale[li], (lam_q1[li], lam_k1[li], lam_q2[li], lam_k2[li]), diff_g[li],
              wq_b, wo_b, wg_b, wu_b, wd_b)
        mk2d, mv2d = _memkv(mem_prompt.reshape(bp * mem_len, D_MODEL), g_mem[li],
                            wk_b, wv_b, li, _pick(bp * mem_len, 256))
        mk = mk2d.reshape(bp, mem_len, MEM_HEADS, MEM_HEAD_DIM)
        mv = mv2d.reshape(bp, mem_len, MEM_HEADS, MEM_HEAD_DIM)
        zero_buf = jnp.zeros((xp.shape[0], POOL_STATE, POOL_DIM), xp.dtype)
        last = li == depth - 1
        xp, p_leaves, npool = _layer(xp, zero_buf, None, None, (mk, mv), lw, li, lam_init,
                                     p_leaves, last)
        p_pool.append(npool); p_mk.append(mk); p_mv.append(mv)
        xs, s_leaves, npool = _layer(xs, state_pool[li], (sb_kT, sb_vT, li),
                                     (diff_k_rows, diff_v_rows, li),
                                     (cache_mem_k, cache_mem_v, li), lw, li, lam_init, s_leaves,
                                     last)
        s_pool.append(npool)
    return (xp, xs,
            *_finish_leaves(p_leaves, xp.shape[0], xp.shape[1]),
            jnp.stack(p_pool), jnp.stack(p_mk), jnp.stack(p_mv),
            *_finish_leaves(s_leaves, xs.shape[0], xs.shape[1]),
            jnp.stack(s_pool))
```
